```python
import jax, jax.numpy as jnp
from jax import lax
import numpy as np

D_MODEL = 1024
BATCH = 2
SEQ = 8192
DEPTH = 1
DEC_BATCH = 128
DEC_SEQ = 8
PAST_LEN = 8192
PAGE_SIZE = 128

GM_WIDTH = D_MODEL // 2
GM_GROUPS = 4
GM_GD = GM_WIDTH // GM_GROUPS
CHUNK = 128
HEAD_DIM = 64
NSA_WIDTH = D_MODEL - GM_WIDTH
NSA_HEADS = NSA_WIDTH // HEAD_DIM
KV_GROUPS = 2
HPG = NSA_HEADS // KV_GROUPS
KVW = KV_GROUPS * HEAD_DIM
N_KV_BRANCH = 6
N_GATES = 3
CMP_LEN = 32
CMP_STRIDE = 16
CMP_RATIO = CMP_LEN // CMP_STRIDE
CMP_HIDDEN = 256
SLC_BLOCK = 64
SEG_PER_SLC = SLC_BLOCK // CMP_STRIDE
TOP_N = 16
WINDOW = 512
Q_BLOCK = 128
FORCE_SCORE = 1e9
D_PROJ = 2 * GM_WIDTH + NSA_WIDTH + N_KV_BRANCH * KVW + N_GATES * NSA_HEADS
SPLITS = [GM_WIDTH, 2 * GM_WIDTH, 2 * GM_WIDTH + NSA_WIDTH, 2 * GM_WIDTH + NSA_WIDTH + N_KV_BRANCH * KVW]
D_FF = ((8 * D_MODEL // 3 + 127) // 128) * 128
CONV_W = 3
EPS = 1e-6

kernel_name = "hymba_gmlp_nsa_convffn_step"


def rmsnorm(x, g):
    xf = x.astype(jnp.float32)
    r = lax.rsqrt(jnp.mean(xf * xf, axis=-1, keepdims=True) + EPS)
    return (xf * r).astype(x.dtype) * g


def layernorm(x, g, b):
    xf = x.astype(jnp.float32)
    mu = jnp.mean(xf, axis=-1, keepdims=True)
    var = jnp.mean((xf - mu) ** 2, axis=-1, keepdims=True)
    return ((xf - mu) * lax.rsqrt(var + EPS)).astype(x.dtype) * g + b


def masked_softmax(s, mask):
    s = jnp.where(mask, s.astype(jnp.float32), -jnp.inf)
    m = jnp.max(s, axis=-1, keepdims=True)
    m = jnp.where(jnp.isfinite(m), m, 0.0)
    e = jnp.exp(s - m)
    return e / jnp.maximum(jnp.sum(e, axis=-1, keepdims=True), 1e-30)


def project(x, norm_g, w_in):
    h = rmsnorm(x, norm_g)
    p = h @ w_in
    u, v, q, kv, gl = jnp.split(p, SPLITS, axis=-1)
    B, T = x.shape[:2]
    q = q.reshape(B, T, NSA_HEADS, HEAD_DIM)
    kv = kv.reshape(B, T, N_KV_BRANCH, KV_GROUPS, HEAD_DIM)
    gates = jax.nn.sigmoid(gl.reshape(B, T, NSA_HEADS, N_GATES))
    return u, v, q, kv, gates


def gmlp_mix(u, v, ln_g, ln_b, w_s, b_s):
    B, T, _ = u.shape
    tc = min(T, CHUNK)
    nc = T // tc
    u = jax.nn.gelu(u)
    vg = jax.nn.gelu(v).reshape(B, T, GM_GROUPS, GM_GD)
    vn = layernorm(vg, ln_g.reshape(GM_GROUPS, GM_GD), ln_b.reshape(GM_GROUPS, GM_GD))
    causal = jnp.tril(jnp.ones((tc, tc), dtype=bool))
    ws = jnp.where(causal, w_s[:, :tc, :tc], 0.0).astype(vn.dtype)
    mixed = jnp.einsum('gts,bcsgd->bctgd', ws, vn.reshape(B, nc, tc, GM_GROUPS, GM_GD))
    mixed = mixed + b_s[:, :tc].T[None, None, :, :, None]
    return u * mixed.reshape(B, T, GM_WIDTH), vn.reshape(B, T, GM_WIDTH)


def compress(k, pe, w1, w2):
    B, L = k.shape[:2]
    nseg = L // CMP_STRIDE
    nc = nseg - CMP_RATIO + 1
    seg = k[:, :nseg * CMP_STRIDE].reshape(B, nseg, CMP_STRIDE, KV_GROUPS, HEAD_DIM)
    w1r = w1.reshape(CMP_RATIO, CMP_STRIDE, HEAD_DIM, CMP_HIDDEN)
    h = jnp.einsum('bnsgd,rsdh->rbngh', seg, w1r)
    hid = sum(h[r, :, r:r + nc] for r in range(CMP_RATIO)) + jnp.einsum('ld,ldh->h', pe, w1)
    return jnp.einsum('bngh,hd->bngd', jax.nn.gelu(hid), w2)


def to_blocks(k):
    B, L = k.shape[:2]
    n_slc = -(-L // SLC_BLOCK)
    k = jnp.pad(k, ((0, 0), (0, n_slc * SLC_BLOCK - L), (0, 0), (0, 0)))
    return k.reshape(B, n_slc, SLC_BLOCK, KV_GROUPS, HEAD_DIM).transpose(0, 3, 1, 2, 4)


def block_importance(p, n_slc):
    nc = p.shape[-1]
    nseg = n_slc * SEG_PER_SLC
    pads = [(0, 0)] * (p.ndim - 1)
    seg = sum(jnp.pad(p, pads + [(r, nseg - nc - r)]) for r in range(CMP_RATIO))
    return seg.reshape(p.shape[:-1] + (n_slc, SEG_PER_SLC)).sum(-1)


def nsa_attend(q, gates, qpos, kc, vc, ks, vs, kw, vw, wpos):
    B, Tq = q.shape[:2]
    f32 = jnp.float32
    qg = q.reshape(B, Tq, KV_GROUPS, HPG, HEAD_DIM).astype(f32) * (HEAD_DIM ** -0.5)
    nc = kc.shape[1]
    cpos = jnp.arange(nc) * CMP_STRIDE + (CMP_LEN - 1)
    s_c = jnp.einsum('btghd,bngd->bghtn', qg, kc.astype(f32))
    p_c = masked_softmax(s_c, cpos[None, :] <= qpos[:, None])
    o_c = jnp.einsum('bghtn,bngd->btghd', p_c, vc.astype(f32))
    n_slc = ks.shape[2]
    imp = block_importance(p_c.sum(axis=2), n_slc)
    blk = jnp.arange(n_slc)[None, :]
    cur = (qpos // SLC_BLOCK)[:, None]
    valid = blk <= cur
    forced = (blk == 0) | (blk == cur) | (blk == cur - 1)
    score = jnp.where(forced, FORCE_SCORE, jnp.where(valid, imp, -jnp.inf))
    n_top = min(TOP_N, n_slc)
    _, idx = lax.top_k(score, n_top)
    bi = jnp.arange(B)[:, None, None, None]
    gi = jnp.arange(KV_GROUPS)[None, :, None, None]
    k_sel = ks[bi, gi, idx].astype(f32)
    v_sel = vs[bi, gi, idx].astype(f32)
    kpos = idx[..., None] * SLC_BLOCK + jnp.arange(SLC_BLOCK)
    mask_s = (kpos <= qpos[None, None, :, None, None]).reshape(B, KV_GROUPS, 1, Tq, n_top * SLC_BLOCK)
    s_s = jnp.einsum('btghd,bgtnsd->bghtns', qg, k_sel).reshape(B, KV_GROUPS, HPG, Tq, n_top * SLC_BLOCK)
    p_s = masked_softmax(s_s, mask_s).reshape(B, KV_GROUPS, HPG, Tq, n_top, SLC_BLOCK)
    o_s = jnp.einsum('bghtns,bgtnsd->btghd', p_s, v_sel)
    d = qpos[:, None] - wpos[None, :]
    mask_w = (d >= 0) & (d < WINDOW) & (wpos[None, :] >= 0)
    s_w = jnp.einsum('btghd,bwgd->bghtw', qg, kw.astype(f32))
    p_w = masked_softmax(s_w, mask_w)
    o_w = jnp.einsum('bghtw,bwgd->btghd', p_w, vw.astype(f32))
    g = gates.reshape(B, Tq, KV_GROUPS, HPG, N_GATES).astype(f32)
    o = o_c * g[..., 0:1] + o_s * g[..., 1:2] + o_w * g[..., 2:3]
    return o.reshape(B, Tq, NSA_WIDTH).astype(q.dtype)


def nsa_prompt(q, gates, kv, cmp_pe, cmp_w1, cmp_w2):
    B, T = q.shape[:2]
    kc = compress(kv[:, :, 0], cmp_pe[0], cmp_w1[0], cmp_w2[0])
    vc = compress(kv[:, :, 1], cmp_pe[1], cmp_w1[1], cmp_w2[1])
    ks = to_blocks(kv[:, :, 2])
    vs = to_blocks(kv[:, :, 3])
    pad = ((0, 0), (WINDOW, 0), (0, 0), (0, 0))
    kw_pad = jnp.pad(kv[:, :, 4], pad)
    vw_pad = jnp.pad(kv[:, :, 5], pad)
    qb = min(Q_BLOCK, T)
    nqb = T // qb
    q_blocks = jnp.moveaxis(q.reshape(B, nqb, qb, NSA_HEADS, HEAD_DIM), 1, 0)
    g_blocks = jnp.moveaxis(gates.reshape(B, nqb, qb, NSA_HEADS, N_GATES), 1, 0)
    starts = jnp.arange(nqb, dtype=jnp.int32) * qb

    def one_block(args):
        qi, gi, s = args
        qpos = s + jnp.arange(qb)
        kw = lax.dynamic_slice_in_dim(kw_pad, s, WINDOW + qb, axis=1)
        vw = lax.dynamic_slice_in_dim(vw_pad, s, WINDOW + qb, axis=1)
        wpos = s - WINDOW + jnp.arange(WINDOW + qb)
        return nsa_attend(qi, gi, qpos, kc, vc, ks, vs, kw, vw, wpos)

    o = lax.map(one_block, (q_blocks, g_blocks, starts))
    return jnp.moveaxis(o, 0, 1).reshape(B, T, NSA_WIDTH)


def nsa_sample(q, gates, kv, past_kv, win_buf, cmp_pe, cmp_w1, cmp_w2):
    T = q.shape[1]
    past = past_kv.shape[1]
    wb = win_buf.shape[1]
    full = jnp.concatenate([past_kv, kv[:, :, :4]], axis=1)
    kc = compress(full[:, :, 0], cmp_pe[0], cmp_w1[0], cmp_w2[0])
    vc = compress(full[:, :, 1], cmp_pe[1], cmp_w1[1], cmp_w2[1])
    ks = to_blocks(full[:, :, 2])
    vs = to_blocks(full[:, :, 3])
    win = jnp.concatenate([win_buf, kv[:, :, 4:]], axis=1)
    qpos = past + jnp.arange(T)
    wpos = past - wb + jnp.arange(wb + T)
    o = nsa_attend(q, gates, qpos, kc, vc, ks, vs, win[:, :, 0], win[:, :, 1], wpos)
    return o, win[:, T:]


def conv_ffn(h, prev, w_up, conv_w, conv_b, w_down):
    a = h @ w_up
    T = a.shape[1]
    ext = jnp.concatenate([prev, a], axis=1)
    c = sum(ext[:, i:i + T] * conv_w[i] for i in range(CONV_W)) + conv_b
    gate, val = jnp.split(c, 2, axis=-1)
    return (jax.nn.gelu(gate) * val) @ w_down, ext[:, T:]


def setup_inputs(seed: int = 0) -> dict:
    key = jax.random.key(seed)
    ks = jax.random.split(key, 24)
    n_pages = PAST_LEN // PAGE_SIZE
    n_phys = (DEC_BATCH * n_pages * 5) // 4
    w_buf = min(WINDOW, PAST_LEN)
    nrm = jax.random.normal
    f32 = jnp.float32
    page_table = jax.random.permutation(ks[5], n_phys)[:DEC_BATCH * n_pages].reshape(DEC_BATCH, n_pages).astype(jnp.int32)
    return {
        "x_prompt": nrm(ks[0], (BATCH, SEQ, D_MODEL), f32),
        "x_sample": nrm(ks[1], (DEC_BATCH, DEC_SEQ, D_MODEL), f32),
        "cache_kv": nrm(ks[2], (DEPTH, n_phys, PAGE_SIZE, 4, KV_GROUPS, HEAD_DIM), f32),
        "cache_win": nrm(ks[3], (DEPTH, DEC_BATCH, w_buf, 2, KV_GROUPS, HEAD_DIM), f32),
        "state_conv": nrm(ks[4], (DEPTH, DEC_BATCH, CONV_W - 1, 2 * D_FF), f32),
        "page_table": page_table,
        "norm1_g": 1.0 + 0.01 * nrm(ks[6], (DEPTH, D_MODEL), f32),
        "w_in": nrm(ks[7], (DEPTH, D_MODEL, D_PROJ), f32) * D_MODEL ** -0.5,
        "gm_ln_g": 1.0 + 0.01 * nrm(ks[8], (DEPTH, GM_WIDTH), f32),
        "gm_ln_b": 0.01 * nrm(ks[9], (DEPTH, GM_WIDTH), f32),
        "gm_ws": nrm(ks[10], (DEPTH, GM_GROUPS, CHUNK, CHUNK), f32) * CHUNK ** -0.5,
        "gm_bs": 1.0 + 0.01 * nrm(ks[11], (DEPTH, GM_GROUPS, CHUNK), f32),
        "cmp_pe": 0.02 * nrm(ks[12], (DEPTH, 2, CMP_LEN, HEAD_DIM), f32),
        "cmp_w1": nrm(ks[13], (DEPTH, 2, CMP_LEN, HEAD_DIM, CMP_HIDDEN), f32) * (CMP_LEN * HEAD_DIM) ** -0.5,
        "cmp_w2": nrm(ks[14], (DEPTH, 2, CMP_HIDDEN, HEAD_DIM), f32) * CMP_HIDDEN ** -0.5,
        "w_out": nrm(ks[15], (DEPTH, D_MODEL, D_MODEL), f32) * D_MODEL ** -0.5,
        "norm2_g": 1.0 + 0.01 * nrm(ks[16], (DEPTH, D_MODEL), f32),
        "w_up": nrm(ks[17], (DEPTH, D_MODEL, 2 * D_FF), f32) * D_MODEL ** -0.5,
        "conv_w": nrm(ks[18], (DEPTH, CONV_W, 2 * D_FF), f32) * CONV_W ** -0.5,
        "conv_b": 0.01 * nrm(ks[19], (DEPTH, 2 * D_FF), f32),
        "w_down": nrm(ks[20], (DEPTH, D_FF, D_MODEL), f32) * D_FF ** -0.5,
        "final_g": 1.0 + 0.01 * nrm(ks[21], (D_MODEL,), f32),
    }


def reference(x_prompt, x_sample, cache_kv, cache_win, state_conv, page_table,
              norm1_g, w_in, gm_ln_g, gm_ln_b, gm_ws, gm_bs, cmp_pe, cmp_w1, cmp_w2, w_out,
              norm2_g, w_up, conv_w, conv_b, w_down, final_g):
    hp, hs = x_prompt, x_sample
    B, T = hp.shape[:2]
    Bd = hs.shape[0]
    wp = min(WINDOW, T)
    kv_p, kv_s, win_p, win_s, conv_p, conv_s, gv_s = [], [], [], [], [], [], []
    for l in range(DEPTH):
        u, v, q, kv, g = project(hp, norm1_g[l], w_in[l])
        a_out, _ = gmlp_mix(u, v, gm_ln_g[l], gm_ln_b[l], gm_ws[l], gm_bs[l])
        n_out = nsa_prompt(q, g, kv, cmp_pe[l], cmp_w1[l], cmp_w2[l])
        hp = hp + jnp.concatenate([a_out, n_out], axis=-1) @ w_out[l]
        prev0 = jnp.zeros((B, CONV_W - 1, 2 * D_FF), hp.dtype)
        f, cnew = conv_ffn(rmsnorm(hp, norm2_g[l]), prev0, w_up[l], conv_w[l], conv_b[l], w_down[l])
        hp = hp + f
        kv_p.append(kv[:, :, :4])
        win_p.append(kv[:, T - wp:, 4:])
        conv_p.append(cnew)
        u, v, q, kv, g = project(hs, norm1_g[l], w_in[l])
        a_out, vn = gmlp_mix(u, v, gm_ln_g[l], gm_ln_b[l], gm_ws[l], gm_bs[l])
        past = cache_kv[l][page_table]
        past = past.reshape(Bd, -1, 4, KV_GROUPS, HEAD_DIM)
        n_out, wnew = nsa_sample(q, g, kv, past, cache_win[l], cmp_pe[l], cmp_w1[l], cmp_w2[l])
        hs = hs + jnp.concatenate([a_out, n_out], axis=-1) @ w_out[l]
        f, cnew = conv_ffn(rmsnorm(hs, norm2_g[l]), state_conv[l], w_up[l], conv_w[l], conv_b[l], w_down[l])
        hs = hs + f
        kv_s.append(kv[:, :, :4])
        win_s.append(wnew)
        conv_s.append(cnew)
        gv_s.append(vn)
    y_prompt = rmsnorm(hp, final_g)
    y_sample = rmsnorm(hs, final_g)
    return (y_prompt, y_sample, jnp.stack(kv_p), jnp.stack(kv_s), jnp.stack(win_p), jnp.stack(win_s),
            jnp.stack(conv_p), jnp.stack(conv_s), jnp.stack(gv_s))
```

```python
import functools
import math

import numpy as np
import jax
import jax.numpy as jnp
from jax import lax
from jax.experimental import pallas as pl
from jax.experimental.pallas import tpu as pltpu

D_MODEL = 1024
GM_WIDTH = D_MODEL // 2
GM_GROUPS = 4
GM_GD = GM_WIDTH // GM_GROUPS
CHUNK = 128
HEAD_DIM = 64
NSA_WIDTH = D_MODEL - GM_WIDTH
NSA_HEADS = NSA_WIDTH // HEAD_DIM
KV_GROUPS = 2
HPG = NSA_HEADS // KV_GROUPS
KVW = KV_GROUPS * HEAD_DIM
N_KV_BRANCH = 6
N_GATES = 3
CMP_LEN = 32
CMP_STRIDE = 16
CMP_RATIO = CMP_LEN // CMP_STRIDE
CMP_HIDDEN = 256
SLC_BLOCK = 64
SLC_SHIFT = SLC_BLOCK.bit_length() - 1
SEG_PER_SLC = SLC_BLOCK // CMP_STRIDE
TOP_N = 16
WINDOW = 512
Q_BLOCK = 128
FORCE_SCORE = 1e9
D_FF = ((8 * D_MODEL // 3 + 127) // 128) * 128
CONV_W = 3
EPS = 1e-6
PAGE_SIZE = 128

LANES = 128
SUBLANES = 8
NEG = -1e30
VMEM_LIMIT = 56 * 1024 * 1024

PROJ_TM = 256
CMP_SEGS = 128
SEL_KC = 512
FFN_TM = 512
FFN_FC = 256
PAGES_PER_STEP = 16

f32 = jnp.float32
bf16 = jnp.bfloat16

_NT = (((1,), (1,)), ((), ()))


def _gelu(x):
    c = math.sqrt(2.0 / math.pi)
    return x * (0.5 * (1.0 + jnp.tanh(c * (x + 0.044715 * (x * x * x)))))


def _dot(a, b):
    return jnp.dot(a, b, preferred_element_type=f32)


def _dot_nt(a, b):
    return lax.dot_general(a, b, _NT, preferred_element_type=f32)


def _rms_scale(x):
    return lax.rsqrt(jnp.mean(x * x, axis=-1, keepdims=True) + EPS)


def _project_kernel(x_ref, g1_ref, wuv_ref, wq_ref, wkv_ref, wgl_ref, lng_ref, lnb_ref,
                    wmix_ref, bmix_ref, a_ref, vn_ref, q_ref, kv_ref, kvb_ref, gate_ref):
    x = x_ref[...]
    h = ((x * _rms_scale(x)) * g1_ref[...]).astype(bf16)
    uv = _dot(h, wuv_ref[...])
    tm = x.shape[0]
    for g in range(GM_GROUPS):
        lo, hi = g * GM_GD, (g + 1) * GM_GD
        u = _gelu(uv[:, lo:hi])
        v = _gelu(uv[:, GM_WIDTH + lo:GM_WIDTH + hi])
        mu = jnp.mean(v, axis=-1, keepdims=True)
        d = v - mu
        var = jnp.mean(d * d, axis=-1, keepdims=True)
        vn = (d * lax.rsqrt(var + EPS)) * lng_ref[:, lo:hi] + lnb_ref[:, lo:hi]
        vn_ref[:, lo:hi] = vn
        vnb = vn.astype(bf16)
        for c in range(tm // CHUNK):
            r0, r1 = c * CHUNK, (c + 1) * CHUNK
            mixed = _dot(wmix_ref[g], vnb[r0:r1]) + bmix_ref[g]
            a_ref[r0:r1, lo:hi] = (u[r0:r1] * mixed).astype(a_ref.dtype)
    q_ref[...] = _dot(h, wq_ref[...]).astype(q_ref.dtype)
    kv = _dot(h, wkv_ref[...])
    kv_ref[...] = kv
    kvb_ref[...] = kv.astype(bf16)
    gl = _dot(h, wgl_ref[...])
    gate_ref[...] = 1.0 / (1.0 + jnp.exp(-gl))


def _project(x, g1, wuv, wq, wkv, wgl, lng, lnb, wmix, bmix, q_dtype):
    n = x.shape[0]
    tm = PROJ_TM
    assert n % tm == 0
    row = lambda w: pl.BlockSpec((tm, w), lambda i: (i, 0))
    full = lambda a: pl.BlockSpec(a.shape, lambda i: (0,) * a.ndim)
    kvw = N_KV_BRANCH * KVW
    return pl.pallas_call(
        _project_kernel,
        grid=(n // tm,),
        in_specs=[row(D_MODEL), full(g1), full(wuv), full(wq), full(wkv), full(wgl),
                  full(lng), full(lnb), full(wmix), full(bmix)],
        out_specs=[row(GM_WIDTH), row(GM_WIDTH), row(NSA_HEADS * LANES), row(kvw), row(kvw),
                   row(KV_GROUPS * LANES)],
        out_shape=[jax.ShapeDtypeStruct((n, GM_WIDTH), bf16),
                   jax.ShapeDtypeStruct((n, GM_WIDTH), f32),
                   jax.ShapeDtypeStruct((n, NSA_HEADS * LANES), q_dtype),
                   jax.ShapeDtypeStruct((n, kvw), f32),
                   jax.ShapeDtypeStruct((n, kvw), bf16),
                   jax.ShapeDtypeStruct((n, KV_GROUPS * LANES), f32)],
        compiler_params=pltpu.CompilerParams(dimension_semantics=("arbitrary",),
                                             vmem_limit_bytes=VMEM_LIMIT),
        name="project",
    )(x, g1, wuv, wq, wkv, wgl, lng, lnb, wmix, bmix)


def _cmp_hidden_rows(load_rows, wz_ref, kind):
    accs = [None] * KV_GROUPS
    for s2 in range(CMP_STRIDE // 2):
        lhs = jnp.concatenate([load_rows(2 * s2), load_rows(2 * s2 + 1)], axis=1).astype(bf16)
        for g in range(KV_GROUPS):
            d = _dot(lhs, wz_ref[kind, g, s2])
            accs[g] = d if accs[g] is None else accs[g] + d
    return accs


def _cmp_hidden_kernel(xk_ref, xv_ref, wz_ref, h_ref):
    segs = h_ref.shape[0]
    hw = CMP_RATIO * CMP_HIDDEN
    for kind, x_ref in enumerate((xk_ref, xv_ref)):
        load = lambda s, x_ref=x_ref: x_ref[pl.ds(s, segs, stride=CMP_STRIDE), :]
        accs = _cmp_hidden_rows(load, wz_ref, kind)
        for g in range(KV_GROUPS):
            c0 = (kind * KV_GROUPS + g) * hw
            h_ref[:, c0:c0 + hw] = accs[g]


def _cmp_hidden(kv, wz):
    n = kv.shape[0]
    rows = CMP_SEGS * CMP_STRIDE
    assert n % rows == 0
    hw = 2 * KV_GROUPS * CMP_RATIO * CMP_HIDDEN
    return pl.pallas_call(
        _cmp_hidden_kernel,
        grid=(n // rows,),
        in_specs=[pl.BlockSpec((rows, LANES), lambda i: (i, 0)),
                  pl.BlockSpec((rows, LANES), lambda i: (i, 1)),
                  pl.BlockSpec(wz.shape, lambda i: (0,) * wz.ndim)],
        out_specs=pl.BlockSpec((CMP_SEGS, hw), lambda i: (i, 0)),
        out_shape=jax.ShapeDtypeStruct((n // CMP_STRIDE, hw), f32),
        compiler_params=pltpu.CompilerParams(dimension_semantics=("arbitrary",),
                                             vmem_limit_bytes=VMEM_LIMIT),
        name="cmp_hidden",
    )(kv, kv, wz)


def _cmp_bias_kernel(pe_ref, w1_ref, o_ref):
    for kind in range(2):
        o_ref[kind] = _dot(pe_ref[kind].astype(bf16), w1_ref[kind].astype(bf16))


def _cmp_bias(pe8, w1flat):
    return pl.pallas_call(
        _cmp_bias_kernel,
        out_shape=jax.ShapeDtypeStruct((2, SUBLANES, CMP_HIDDEN), f32),
        name="cmp_bias",
    )(pe8, w1flat)


def _cmp_finalize(hfull, bias_ref, w2p_ref):
    nseg = hfull.shape[0]
    hw = CMP_RATIO * CMP_HIDDEN
    outs = []
    for kind in range(2):
        acc = None
        for g in range(KV_GROUPS):
            c0 = (kind * KV_GROUPS + g) * hw
            h0 = hfull[:, c0:c0 + CMP_HIDDEN]
            h1 = hfull[:, c0 + CMP_HIDDEN:c0 + hw]
            hid = h0 + pltpu.roll(h1, nseg - 1, axis=0) + bias_ref[kind, 0:1, :]
            d = _dot(_gelu(hid).astype(bf16), w2p_ref[kind, g])
            acc = d if acc is None else acc + d
        outs.append(acc)
    return outs


def _cmp_final_kernel(h_ref, bias_ref, w2p_ref, kc_ref, vc_ref):
    kc, vc = _cmp_finalize(h_ref, bias_ref, w2p_ref)
    kc_ref[...] = kc.astype(bf16)
    vc_ref[...] = vc.astype(bf16)


def _cmp_final(hid, bias, w2p, batch):
    nseg = hid.shape[0] // batch
    hw = hid.shape[1]
    full = lambda a: pl.BlockSpec(a.shape, lambda b: (0,) * a.ndim)
    out = jax.ShapeDtypeStruct((batch, nseg, LANES), bf16)
    return pl.pallas_call(
        _cmp_final_kernel,
        grid=(batch,),
        in_specs=[pl.BlockSpec((nseg, hw), lambda b: (b, 0)), full(bias), full(w2p)],
        out_specs=[pl.BlockSpec((None, nseg, LANES), lambda b: (b, 0, 0))] * 2,
        out_shape=[out, out],
        compiler_params=pltpu.CompilerParams(dimension_semantics=("arbitrary",),
                                             vmem_limit_bytes=VMEM_LIMIT),
        name="cmp_final",
    )(hid, bias, w2p)


def _softmax_rows(s, mask):
    s = jnp.where(mask, s, NEG)
    m = jnp.max(s, axis=-1, keepdims=True)
    p = jnp.where(mask, jnp.exp(s - m), 0.0)
    l = jnp.sum(p, axis=-1, keepdims=True)
    return p, 1.0 / jnp.maximum(l, 1e-30)


def _importance(psum, m_ref):
    hi = psum.astype(bf16)
    lo = (psum - hi.astype(f32)).astype(bf16)
    return _dot(hi, m_ref[...]) + _dot(lo, m_ref[...])


def _select_top(score, rounds):
    lane = lax.broadcasted_iota(jnp.int32, score.shape, 1).astype(f32)

    def body(_, carry):
        s, sel = carry
        m = jnp.max(s, axis=-1, keepdims=True)
        idx = jnp.min(jnp.where(s == m, lane, float(score.shape[1])), axis=-1, keepdims=True)
        pick = lane == idx
        return jnp.where(pick, -jnp.inf, s), jnp.where(pick, 1.0, sel)

    _, sel = lax.fori_loop(0, rounds, body, (score, jnp.zeros(score.shape, f32)))
    return sel


def _attn_prompt_kernel(q_ref, gate_ref, kc_ref, vc_ref, ks_ref, vs_ref, kw_ref, vw_ref,
                        m_ref, e_ref, o_ref, acc_sc, m_sc, l_sc, *, nc):
    i = pl.program_id(2)
    qb = Q_BLOCK
    rows = HPG * qb
    q0 = i * qb
    qblk = q_ref[...]
    q = jnp.concatenate([qblk[:, h * LANES:(h + 1) * LANES] for h in range(HPG)], axis=0)
    qpos = q0 + (lax.broadcasted_iota(jnp.int32, (rows, 1), 0) & (qb - 1))

    kc = kc_ref[...]
    nseg = kc.shape[0]
    n_idx = lax.broadcasted_iota(jnp.int32, (rows, nseg), 1)
    cmask = (n_idx * CMP_STRIDE + (CMP_LEN - 1) <= qpos) & (n_idx < nc)
    p, inv = _softmax_rows(_dot_nt(q, kc), cmask)
    p = p * inv
    o_c = _dot(p.astype(bf16), vc_ref[...])

    psum = p[0:qb]
    for h in range(1, HPG):
        psum = psum + p[h * qb:(h + 1) * qb]
    imp = _importance(psum, m_ref)
    n_slc = imp.shape[1]
    blk = lax.broadcasted_iota(jnp.int32, (qb, n_slc), 1)
    cur = jnp.right_shift(q0 + lax.broadcasted_iota(jnp.int32, (qb, n_slc), 0), SLC_SHIFT)
    forced = (blk == 0) | (blk == cur) | (blk == cur - 1)
    score = jnp.where(forced, FORCE_SCORE, jnp.where(blk <= cur, imp, -jnp.inf))
    sel = _select_top(score, min(TOP_N, n_slc)).astype(bf16)

    acc_sc[...] = jnp.zeros(acc_sc.shape, f32)
    m_sc[...] = jnp.full(m_sc.shape, NEG, f32)
    l_sc[...] = jnp.zeros(l_sc.shape, f32)
    kc_len = SEL_KC

    def body(c, carry):
        k0 = pl.multiple_of(c * kc_len, kc_len)
        s = _dot_nt(q, ks_ref[pl.ds(k0, kc_len), :])
        sel_e = _dot(sel, e_ref[:, pl.ds(k0, kc_len)])
        sel_e = jnp.concatenate([sel_e] * HPG, axis=0)
        kpos = k0 + lax.broadcasted_iota(jnp.int32, (rows, kc_len), 1)
        mask = (sel_e > 0.5) & (kpos <= qpos)
        s = jnp.where(mask, s, NEG)
        m_old = m_sc[...]
        m_new = jnp.maximum(m_old, jnp.max(s, axis=-1, keepdims=True))
        alpha = jnp.exp(m_old - m_new)
        pp = jnp.where(mask, jnp.exp(s - m_new), 0.0)
        l_sc[...] = alpha * l_sc[...] + jnp.sum(pp, axis=-1, keepdims=True)
        acc_sc[...] = alpha * acc_sc[...] + _dot(pp.astype(bf16), vs_ref[pl.ds(k0, kc_len), :])
        m_sc[...] = m_new
        return carry

    lax.fori_loop(0, (q0 + qb + kc_len - 1) // kc_len, body, 0)
    o_s = acc_sc[...] * (1.0 / jnp.maximum(l_sc[...], 1e-30))

    wlen = WINDOW + qb
    w0 = pl.multiple_of(jnp.maximum(q0 - WINDOW, 0), qb)
    dist = qpos - (w0 + lax.broadcasted_iota(jnp.int32, (rows, wlen), 1))
    pw, invw = _softmax_rows(_dot_nt(q, kw_ref[pl.ds(w0, wlen), :]), (dist >= 0) & (dist < WINDOW))
    o_w = _dot(pw.astype(bf16), vw_ref[pl.ds(w0, wlen), :]) * invw

    gt = gate_ref[...]
    for h in range(HPG):
        r0, r1 = h * qb, (h + 1) * qb
        c = h * N_GATES
        o = (o_c[r0:r1] * gt[:, c:c + 1] + o_s[r0:r1] * gt[:, c + 1:c + 2]
             + o_w[r0:r1] * gt[:, c + 2:c + 3])
        o_ref[:, h * LANES:(h + 1) * LANES] = o.astype(o_ref.dtype)


def _attn_prompt(qpad, gates, kc, vc, kvb, imp_m, sel_e, batch):
    n = qpad.shape[0]
    t = n // batch
    qb = Q_BLOCK
    nqb = t // qb
    assert t % SEL_KC == 0 and t >= WINDOW + qb
    nseg = t // CMP_STRIDE
    nc = nseg - CMP_RATIO + 1
    gw = HPG * LANES
    col = lambda c: pl.BlockSpec((t, LANES), lambda b, g, i, c=c: (b, c))
    cmp_spec = pl.BlockSpec((None, nseg, LANES), lambda b, g, i: (b, 0, 0))
    full = lambda a: pl.BlockSpec(a.shape, lambda b, g, i: (0,) * a.ndim)
    rows = HPG * qb
    return pl.pallas_call(
        functools.partial(_attn_prompt_kernel, nc=nc),
        grid=(batch, KV_GROUPS, nqb),
        in_specs=[pl.BlockSpec((qb, gw), lambda b, g, i: (b * nqb + i, g)),
                  pl.BlockSpec((qb, LANES), lambda b, g, i: (b * nqb + i, g)),
                  cmp_spec, cmp_spec, col(2), col(3), col(4), col(5),
                  full(imp_m), full(sel_e)],
        out_specs=pl.BlockSpec((qb, gw), lambda b, g, i: (b * nqb + i, g)),
        out_shape=jax.ShapeDtypeStruct((n, NSA_HEADS * LANES), bf16),
        scratch_shapes=[pltpu.VMEM((rows, LANES), f32), pltpu.VMEM((rows, 1), f32),
                        pltpu.VMEM((rows, 1), f32)],
        compiler_params=pltpu.CompilerParams(
            dimension_semantics=("arbitrary", "arbitrary", "arbitrary"),
            vmem_limit_bytes=VMEM_LIMIT),
        name="attn_prompt",
    )(qpad, gates, kc, vc, kvb, kvb, kvb, kvb, imp_m, sel_e)


def _attn_sample_kernel(pt_ref, *refs, past, tdec):
    del pt_ref
    npg = PAGES_PER_STEP
    pages = refs[:npg]
    (q_ref, gate_ref, kv_ref, win_ref, wz_ref, bias_ref, w2p_ref, m_ref, e_ref,
     o_ref, h_sc, ks_sc, vs_sc) = refs[npg:]
    j = pl.program_id(1)
    nj = pl.num_programs(1)
    segs_pp = PAGE_SIZE // CMP_STRIDE
    segs = npg * segs_pp
    hw = CMP_RATIO * CMP_HIDDEN

    seg0 = pl.multiple_of(j * segs, segs)
    for kind in range(2):
        load = lambda s, kind=kind: jnp.concatenate(
            [pg[pl.ds(4 * s + kind, segs_pp, stride=4 * CMP_STRIDE), :] for pg in pages], axis=0)
        accs = _cmp_hidden_rows(load, wz_ref, kind)
        for g in range(KV_GROUPS):
            c0 = (kind * KV_GROUPS + g) * hw
            h_sc[pl.ds(seg0, segs), c0:c0 + hw] = accs[g]

    for pi, pg in enumerate(pages):
        r0 = pl.multiple_of((j * npg + pi) * PAGE_SIZE, PAGE_SIZE)
        ks_sc[pl.ds(r0, PAGE_SIZE), :] = pg[pl.ds(2, PAGE_SIZE, stride=4), :].astype(bf16)
        vs_sc[pl.ds(r0, PAGE_SIZE), :] = pg[pl.ds(3, PAGE_SIZE, stride=4), :].astype(bf16)

    @pl.when(j == nj - 1)
    def _():
        nh = NSA_HEADS
        rows = nh * tdec
        nseg = past // CMP_STRIDE
        nc = (past + tdec) // CMP_STRIDE - CMP_RATIO + 1
        kc, vc = _cmp_finalize(h_sc, bias_ref, w2p_ref)
        qf = q_ref[...]
        q = jnp.concatenate([qf[:, h * LANES:(h + 1) * LANES] for h in range(nh)],
                            axis=0).astype(bf16)
        trow = lax.broadcasted_iota(jnp.int32, (rows, 1), 0) & (tdec - 1)
        qpos = past + trow

        n_idx = lax.broadcasted_iota(jnp.int32, (rows, nseg), 1)
        cmask = (n_idx * CMP_STRIDE + (CMP_LEN - 1) <= qpos) & (n_idx < nc)
        p, inv = _softmax_rows(_dot_nt(q, kc.astype(bf16)), cmask)
        p = p * inv
        o_c = _dot(p.astype(bf16), vc.astype(bf16))

        gr = HPG * tdec
        psums = []
        for g in range(KV_GROUPS):
            ps = p[g * gr:g * gr + tdec]
            for h in range(1, HPG):
                ps = ps + p[g * gr + h * tdec:g * gr + (h + 1) * tdec]
            psums.append(ps)
        imp = _importance(jnp.concatenate(psums, axis=0), m_ref)
        n_past_blk = past // SLC_BLOCK
        blk = lax.broadcasted_iota(jnp.int32, imp.shape, 1)
        forced = (blk == 0) | (blk == n_past_blk - 1)
        sel = _select_top(jnp.where(forced, FORCE_SCORE, imp), TOP_N - 1)
        sel_rows = jnp.concatenate(
            [sel[g * tdec:(g + 1) * tdec] for g in range(KV_GROUPS) for _ in range(HPG)], axis=0)

        def new_rows(c):
            blk_new = kv_ref[:, c * LANES:(c + 1) * LANES].astype(bf16)
            return jnp.concatenate([blk_new, jnp.zeros((LANES - tdec, LANES), bf16)], axis=0)

        new_mask = lax.broadcasted_iota(jnp.int32, (rows, LANES), 1) <= trow

        def two_part(s_old, mask_old, v_old, c_k, c_v):
            s_new = _dot_nt(q, new_rows(c_k))
            s_old = jnp.where(mask_old, s_old, NEG)
            s_new = jnp.where(new_mask, s_new, NEG)
            m = jnp.maximum(jnp.max(s_old, axis=-1, keepdims=True),
                            jnp.max(s_new, axis=-1, keepdims=True))
            p_old = jnp.where(mask_old, jnp.exp(s_old - m), 0.0)
            p_new = jnp.where(new_mask, jnp.exp(s_new - m), 0.0)
            l = jnp.sum(p_old, axis=-1, keepdims=True) + jnp.sum(p_new, axis=-1, keepdims=True)
            o = _dot(p_old.astype(bf16), v_old) + _dot(p_new.astype(bf16), new_rows(c_v))
            return o * (1.0 / jnp.maximum(l, 1e-30))

        sel_e = _dot(sel_rows.astype(bf16), e_ref[...])
        o_s = two_part(_dot_nt(q, ks_sc[...]), sel_e > 0.5, vs_sc[...], 2, 3)

        wb = win_ref.shape[0]
        dist = wb + trow - lax.broadcasted_iota(jnp.int32, (rows, wb), 1)
        kw = win_ref[:, 0:LANES].astype(bf16)
        vw = win_ref[:, LANES:2 * LANES].astype(bf16)
        o_w = two_part(_dot_nt(q, kw), (dist >= 0) & (dist < WINDOW), vw, 4, 5)

        gt = gate_ref[...]
        for h in range(nh):
            r0, r1 = h * tdec, (h + 1) * tdec
            c = (h // HPG) * LANES + (h % HPG) * N_GATES
            o = (o_c[r0:r1] * gt[:, c:c + 1] + o_s[r0:r1] * gt[:, c + 1:c + 2]
                 + o_w[r0:r1] * gt[:, c + 2:c + 3])
            o_ref[:, h * LANES:(h + 1) * LANES] = o.astype(o_ref.dtype)


def _attn_sample(page_table, cache, q3, gates3, kv3, win, wz, bias, w2p, imp_m, sel_e):
    bd, n_pages = page_table.shape
    tdec = q3.shape[1]
    past = n_pages * PAGE_SIZE
    npg = PAGES_PER_STEP
    assert n_pages % npg == 0 and past % SLC_BLOCK == 0 and tdec <= CMP_STRIDE
    assert past // SLC_BLOCK == LANES and win.shape[1] <= WINDOW
    nj = n_pages // npg
    nseg = past // CMP_STRIDE
    hw = 2 * KV_GROUPS * CMP_RATIO * CMP_HIDDEN
    page_specs = [
        pl.BlockSpec((None, 4 * PAGE_SIZE, LANES),
                     lambda b, j, pt, p=p: (pt[b * n_pages + j * npg + p], 0, 0))
        for p in range(npg)]
    per_b = lambda a: pl.BlockSpec((None,) + a.shape[1:], lambda b, j, pt: (b,) + (0,) * (a.ndim - 1))
    full = lambda a: pl.BlockSpec(a.shape, lambda b, j, pt: (0,) * a.ndim)
    grid_spec = pltpu.PrefetchScalarGridSpec(
        num_scalar_prefetch=1,
        grid=(bd, nj),
        in_specs=page_specs + [per_b(q3), per_b(gates3), per_b(kv3), per_b(win),
                               full(wz), full(bias), full(w2p), full(imp_m), full(sel_e)],
        out_specs=pl.BlockSpec((None, tdec, NSA_HEADS * LANES), lambda b, j, pt: (b, 0, 0)),
        scratch_shapes=[pltpu.VMEM((nseg, hw), f32), pltpu.VMEM((past, LANES), bf16),
                        pltpu.VMEM((past, LANES), bf16)],
    )
    return pl.pallas_call(
        functools.partial(_attn_sample_kernel, past=past, tdec=tdec),
        grid_spec=grid_spec,
        out_shape=jax.ShapeDtypeStruct((bd, tdec, NSA_HEADS * LANES), f32),
        compiler_params=pltpu.CompilerParams(dimension_semantics=("arbitrary", "arbitrary"),
                                             vmem_limit_bytes=VMEM_LIMIT),
        name="attn_sample",
    )(page_table.reshape(-1), *([cache] * npg), q3, gates3, kv3, win, wz, bias, w2p, imp_m, sel_e)


def _ffn_kernel(*refs, tiles_per_seq, tdec):
    if tdec:
        (x_ref, a_ref, o_ref, wout_ref, g2_ref, gf_ref, wg_ref, wv_ref, cwg_ref, cwv_ref,
         cbg_ref, cbv_ref, wd_ref, pg_ref, pv_ref, y_ref, tg_ref, tv_ref,
         hp_sc, h2_sc, acc_sc) = refs
    else:
        (x_ref, a_ref, o_ref, wout_ref, g2_ref, gf_ref, wg_ref, wv_ref, cwg_ref, cwv_ref,
         cbg_ref, cbv_ref, wd_ref, y_ref, tg_ref, tv_ref,
         hp_sc, h2_sc, acc_sc, cg_sc, cv_sc) = refs
    i = pl.program_id(0)
    j = pl.program_id(1)
    nj = pl.num_programs(1)
    tm = x_ref.shape[0]

    @pl.when(j == 0)
    def _():
        cat = jnp.concatenate([a_ref[...], o_ref[...].astype(bf16)], axis=1)
        hp = x_ref[...] + _dot(cat, wout_ref[...])
        hp_sc[...] = hp
        h2_sc[...] = ((hp * _rms_scale(hp)) * g2_ref[...]).astype(bf16)
        acc_sc[...] = jnp.zeros(acc_sc.shape, f32)

    h2 = h2_sc[...]
    halves = []
    for w_ref, cw_ref, cb_ref, t_ref, extra in (
            (wg_ref, cwg_ref, cbg_ref, tg_ref, pg_ref if tdec else cg_sc),
            (wv_ref, cwv_ref, cbv_ref, tv_ref, pv_ref if tdec else cv_sc)):
        a = _dot(h2, w_ref[...])
        fc = a.shape[1]
        if tdec:
            nb = tm // tdec
            a3 = a.reshape(nb, tdec, fc)
            t_ref[...] = a3[:, tdec - (CONV_W - 1):, :]
            prev = extra[...]
            trow = lax.broadcasted_iota(jnp.int32, (nb, tdec, fc), 1)
            s1 = jnp.where(trow == 0, prev[:, 1:2, :], pltpu.roll(a3, 1, axis=1))
            s2 = jnp.where(trow == 0, prev[:, 0:1, :],
                           jnp.where(trow == 1, prev[:, 1:2, :], pltpu.roll(a3, 2, axis=1)))
            c = (s2 * cw_ref[0:1, :] + s1 * cw_ref[1:2, :] + a3 * cw_ref[2:3, :]
                 + cb_ref[...]).reshape(tm, fc)
        else:
            tail = a[tm - SUBLANES:, :]
            t_ref[...] = tail

            @pl.when(i % tiles_per_seq == 0)
            def _(extra=extra):
                extra[j] = jnp.zeros(extra.shape[1:], f32)

            prev = extra[j]
            row = lax.broadcasted_iota(jnp.int32, (tm, fc), 0)
            s1 = jnp.where(row == 0, prev[SUBLANES - 1:SUBLANES, :], pltpu.roll(a, 1, axis=0))
            s2 = jnp.where(row == 0, prev[SUBLANES - 2:SUBLANES - 1, :],
                           jnp.where(row == 1, prev[SUBLANES - 1:SUBLANES, :],
                                     pltpu.roll(a, 2, axis=0)))
            extra[j] = tail
            c = s2 * cw_ref[0:1, :] + s1 * cw_ref[1:2, :] + a * cw_ref[2:3, :] + cb_ref[...]
        halves.append(c)
    act = (_gelu(halves[0]) * halves[1]).astype(bf16)
    acc_sc[...] += _dot(act, wd_ref[...])

    @pl.when(j == nj - 1)
    def _():
        out = hp_sc[...] + acc_sc[...]
        y_ref[...] = (out * _rms_scale(out)) * gf_ref[...]


def _ffn(x, a_out, o_pad, wout, g2, gf, wup, cw, cb, wdn, state, seq_len, tdec, tm):
    n = x.shape[0]
    fc = FFN_FC
    assert n % tm == 0 and D_FF % fc == 0
    ni, nj = n // tm, D_FF // fc
    row = lambda w: pl.BlockSpec((tm, w), lambda i, j: (i, 0))
    full = lambda a: pl.BlockSpec(a.shape, lambda i, j: (0,) * a.ndim)
    gate_col = lambda r: pl.BlockSpec((r, fc), lambda i, j: (0, j))
    val_col = lambda r: pl.BlockSpec((r, fc), lambda i, j: (0, nj + j))
    in_specs = [row(D_MODEL), row(GM_WIDTH), row(NSA_HEADS * LANES), full(wout), full(g2), full(gf),
                gate_col(D_MODEL), val_col(D_MODEL), gate_col(CONV_W), val_col(CONV_W),
                gate_col(1), val_col(1), pl.BlockSpec((fc, D_MODEL), lambda i, j: (j, 0))]
    args = [x, a_out, o_pad, wout, g2, gf, wup, wup, cw, cw, cb, cb, wdn]
    scratch = [pltpu.VMEM((tm, D_MODEL), f32), pltpu.VMEM((tm, D_MODEL), bf16),
               pltpu.VMEM((tm, D_MODEL), f32)]
    if tdec:
        assert tm % tdec == 0 and tdec == SUBLANES
        nb = tm // tdec
        in_specs += [pl.BlockSpec((nb, CONV_W - 1, fc), lambda i, j: (i, 0, j)),
                     pl.BlockSpec((nb, CONV_W - 1, fc), lambda i, j: (i, 0, nj + j))]
        args += [state, state]
        tail_shape = jax.ShapeDtypeStruct((n // tdec, CONV_W - 1, D_FF), f32)
        tail_spec = pl.BlockSpec((nb, CONV_W - 1, fc), lambda i, j: (i, 0, j))
        tiles_per_seq = 0
    else:
        assert seq_len % tm == 0
        tiles_per_seq = seq_len // tm
        tail_shape = jax.ShapeDtypeStruct((ni, SUBLANES, D_FF), f32)
        tail_spec = pl.BlockSpec((None, SUBLANES, fc), lambda i, j: (i, 0, j))
        scratch += [pltpu.VMEM((nj, SUBLANES, fc), f32), pltpu.VMEM((nj, SUBLANES, fc), f32)]
    return pl.pallas_call(
        functools.partial(_ffn_kernel, tiles_per_seq=tiles_per_seq, tdec=tdec),
        grid=(ni, nj),
        in_specs=in_specs,
        out_specs=[row(D_MODEL), tail_spec, tail_spec],
        out_shape=[jax.ShapeDtypeStruct((n, D_MODEL), f32), tail_shape, tail_shape],
        scratch_shapes=scratch,
        compiler_params=pltpu.CompilerParams(dimension_semantics=("arbitrary", "arbitrary"),
                                             vmem_limit_bytes=VMEM_LIMIT),
        name="ffn_sample" if tdec else "ffn_prompt",
    )(*args)


def _pad_group_lanes(w, heads_axis_len):
    lead = w.shape[:-1]
    w = w.reshape(lead + (heads_axis_len, HEAD_DIM))
    z = jnp.zeros_like(w)
    first = (jnp.arange(heads_axis_len) < HPG)[:, None]
    lo = jnp.where(first, w, z)
    hi = jnp.where(first, z, w)
    return jnp.concatenate([lo, hi], axis=-1).reshape(lead + (heads_axis_len * LANES,))


def _importance_matrix(nseg, n_blk):
    m = np.zeros((nseg, n_blk), np.float32)
    for n in range(nseg - CMP_RATIO + 1):
        for r in range(CMP_RATIO):
            b = (n + r) // SEG_PER_SLC
            if b < n_blk:
                m[n, b] += 1.0
    return jnp.asarray(m, bf16)


def _block_expander(n_blk, n_keys):
    e = (np.arange(n_keys)[None, :] // SLC_BLOCK) == np.arange(n_blk)[:, None]
    return jnp.asarray(e.astype(np.float32), bf16)


def kernel(x_prompt, x_sample, cache_kv, cache_win, state_conv, page_table, norm1_g, w_in, gm_ln_g,
           gm_ln_b, gm_ws, gm_bs, cmp_pe, cmp_w1, cmp_w2, w_out, norm2_g, w_up, conv_w, conv_b,
           w_down, final_g):
    depth = w_in.shape[0]
    assert depth == 1
    l = 0
    B, T, _ = x_prompt.shape
    Bd, Td, _ = x_sample.shape
    assert T % CHUNK == 0 and Td == SUBLANES and CHUNK % Td == 0

    wi = w_in[l]
    o_q = 2 * GM_WIDTH
    o_kv = o_q + NSA_WIDTH
    o_gl = o_kv + N_KV_BRANCH * KVW
    wuv = wi[:, :o_q].astype(bf16)
    wq = _pad_group_lanes(wi[:, o_q:o_kv] * (HEAD_DIM ** -0.5), NSA_HEADS).astype(bf16)
    wkv = wi[:, o_kv:o_gl].astype(bf16)
    gpg = HPG * N_GATES
    wgl = jnp.concatenate(
        [jnp.pad(wi[:, o_gl + g * gpg:o_gl + (g + 1) * gpg], ((0, 0), (0, LANES - gpg)))
         for g in range(KV_GROUPS)], axis=1).astype(bf16)
    g1 = norm1_g[l].reshape(1, D_MODEL)
    lng = gm_ln_g[l].reshape(1, GM_WIDTH)
    lnb = gm_ln_b[l].reshape(1, GM_WIDTH)
    causal = jnp.tril(jnp.ones((CHUNK, CHUNK), bool))
    wmix_p = jnp.where(causal, gm_ws[l], 0.0).astype(bf16)
    bmix_p = jnp.broadcast_to(gm_bs[l][:, :, None], (GM_GROUPS, CHUNK, GM_GD))
    reps = CHUNK // Td
    ws_d = jnp.where(causal[:Td, :Td], gm_ws[l][:, :Td, :Td], 0.0)
    eye = jnp.eye(reps, dtype=f32)
    wmix_s = (eye[None, :, None, :, None] * ws_d[:, None, :, None, :]).reshape(
        GM_GROUPS, CHUNK, CHUNK).astype(bf16)
    bmix_s = jnp.broadcast_to(jnp.tile(gm_bs[l][:, :Td], (1, reps))[:, :, None],
                              (GM_GROUPS, CHUNK, GM_GD))

    w1r = cmp_w1[l].reshape(2, CMP_RATIO, CMP_STRIDE, HEAD_DIM, CMP_HIDDEN)
    w1cat = jnp.concatenate([w1r[:, r] for r in range(CMP_RATIO)], axis=-1)
    zero = jnp.zeros_like(w1cat)
    per_g = jnp.stack([jnp.concatenate([w1cat, zero], axis=2),
                       jnp.concatenate([zero, w1cat], axis=2)], axis=1)
    wz = per_g.reshape(2, KV_GROUPS, CMP_STRIDE // 2, 2 * LANES, CMP_RATIO * CMP_HIDDEN).astype(bf16)
    pe8 = jnp.broadcast_to(cmp_pe[l].reshape(2, 1, CMP_LEN * HEAD_DIM), (2, SUBLANES, CMP_LEN * HEAD_DIM))
    cbias = _cmp_bias(pe8, cmp_w1[l].reshape(2, CMP_LEN * HEAD_DIM, CMP_HIDDEN))
    w2 = cmp_w2[l]
    z2 = jnp.zeros_like(w2)
    w2p = jnp.stack([jnp.concatenate([w2, z2], axis=-1),
                     jnp.concatenate([z2, w2], axis=-1)], axis=1).astype(bf16)

    wo = w_out[l]
    wout = jnp.concatenate([wo[:GM_WIDTH], _pad_group_lanes(wo[GM_WIDTH:].T, NSA_HEADS).T],
                           axis=0).astype(bf16)
    g2 = norm2_g[l].reshape(1, D_MODEL)
    gf = final_g.reshape(1, D_MODEL)
    wup = w_up[l].astype(bf16)
    wdn = w_down[l].astype(bf16)
    cw = conv_w[l]
    cb = conv_b[l].reshape(1, 2 * D_FF)

    xp = x_prompt.reshape(B * T, D_MODEL)
    a_p, _, q_p, kv_p, kvb_p, gate_p = _project(xp, g1, wuv, wq, wkv, wgl, lng, lnb, wmix_p, bmix_p, bf16)
    hid_p = _cmp_hidden(kv_p, wz)
    kc_p, vc_p = _cmp_final(hid_p, cbias, w2p, B)
    imp_m = _importance_matrix(T // CMP_STRIDE, T // SLC_BLOCK)
    o_p = _attn_prompt(q_p, gate_p, kc_p, vc_p, kvb_p, imp_m, _block_expander(T // SLC_BLOCK, T), B)
    y_p, tg_p, tv_p = _ffn(xp, a_p, o_p, wout, g2, gf, wup, cw, cb, wdn, None, T, 0, FFN_TM)
    tiles = T // FFN_TM
    tail_p = jnp.concatenate([tg_p, tv_p], axis=-1).reshape(B, tiles, SUBLANES, 2 * D_FF)
    conv_prompt = tail_p[:, tiles - 1, SUBLANES - (CONV_W - 1):]
    kv6 = kv_p.reshape(B, T, N_KV_BRANCH, KV_GROUPS, HEAD_DIM)
    wp = min(WINDOW, T)

    n_pages = page_table.shape[1]
    past = n_pages * PAGE_SIZE
    xs = x_sample.reshape(Bd * Td, D_MODEL)
    a_s, vn_s, q_s, kv_s, _, gate_s = _project(xs, g1, wuv, wq, wkv, wgl, lng, lnb, wmix_s, bmix_s, f32)
    cache = cache_kv[l].reshape(cache_kv.shape[1], 4 * PAGE_SIZE, KVW)
    wb = cache_win.shape[2]
    win = cache_win[l].reshape(Bd, wb, 2 * KVW)
    o_s = _attn_sample(page_table, cache, q_s.reshape(Bd, Td, -1), gate_s.reshape(Bd, Td, -1),
                       kv_s.reshape(Bd, Td, -1), win, wz, cbias, w2p,
                       _importance_matrix(past // CMP_STRIDE, past // SLC_BLOCK),
                       _block_expander(past // SLC_BLOCK, past))
    tm_s = min(FFN_TM, Bd * Td)
    y_s, tg_s, tv_s = _ffn(xs, a_s, o_s.reshape(Bd * Td, -1), wout, g2, gf, wup, cw, cb, wdn,
                           state_conv[l], Td, Td, tm_s)
    kv6_s = kv_s.reshape(Bd, Td, N_KV_BRANCH, KV_GROUPS, HEAD_DIM)
    win_new = jnp.concatenate([cache_win[l], kv6_s[:, :, 4:]], axis=1)[:, Td:]

    return (y_p.reshape(B, T, D_MODEL),
            y_s.reshape(Bd, Td, D_MODEL),
            kv6[None, :, :, :4],
            kv6_s[None, :, :, :4],
            kv6[None, :, T - wp:, 4:],
            win_new[None],
            conv_prompt[None],
            jnp.concatenate([tg_s, tv_s], axis=-1)[None],
            vn_s.reshape(1, Bd, Td, GM_WIDTH))
```

```python
import functools
import math

import numpy as np
import jax
import jax.numpy as jnp
from jax import lax
from jax.experimental import pallas as pl
from jax.experimental.pallas import tpu as pltpu

D_MODEL = 1024
GM_WIDTH = D_MODEL // 2
GM_GROUPS = 4
GM_GD = GM_WIDTH // GM_GROUPS
CHUNK = 128
HEAD_DIM = 64
NSA_WIDTH = D_MODEL - GM_WIDTH
NSA_HEADS = NSA_WIDTH // HEAD_DIM
KV_GROUPS = 2
HPG = NSA_HEADS // KV_GROUPS
KVW = KV_GROUPS * HEAD_DIM
N_KV_BRANCH = 6
N_GATES = 3
CMP_LEN = 32
CMP_STRIDE = 16
CMP_RATIO = CMP_LEN // CMP_STRIDE
CMP_HIDDEN = 256
SLC_BLOCK = 64
SLC_SHIFT = SLC_BLOCK.bit_length() - 1
SEG_PER_SLC = SLC_BLOCK // CMP_STRIDE
TOP_N = 16
WINDOW = 512
Q_BLOCK = 128
FORCE_SCORE = 1e9
D_FF = ((8 * D_MODEL // 3 + 127) // 128) * 128
CONV_W = 3
EPS = 1e-6
PAGE_SIZE = 128

LANES = 128
SUBLANES = 8
NEG = -1e30
VMEM_LIMIT = 56 * 1024 * 1024

PROJ_TM = 256
CMP_SEGS = 128
SEL_KC = 512
FFN_TM = 512
FFN_FC = 256
PAGES_PER_STEP = 16

f32 = jnp.float32
bf16 = jnp.bfloat16

_NT = (((1,), (1,)), ((), ()))


def _gelu(x):
    c = math.sqrt(2.0 / math.pi)
    return x * (0.5 * (1.0 + jnp.tanh(c * (x + 0.044715 * (x * x * x)))))


def _dot(a, b):
    return jnp.dot(a, b, preferred_element_type=f32)


def _dot_nt(a, b):
    return lax.dot_general(a, b, _NT, preferred_element_type=f32)


def _rms_scale(x):
    return lax.rsqrt(jnp.mean(x * x, axis=-1, keepdims=True) + EPS)


def _project_kernel(x_ref, g1_ref, wuv_ref, wq_ref, wkv_ref, wgl_ref, lng_ref, lnb_ref,
                    wmix_ref, bmix_ref, a_ref, vn_ref, q_ref, kv_ref, kvb_ref, gate_ref):
    x = x_ref[...]
    h = ((x * _rms_scale(x)) * g1_ref[...]).astype(bf16)
    uv = _dot(h, wuv_ref[...])
    tm = x.shape[0]
    for g in range(GM_GROUPS):
        lo, hi = g * GM_GD, (g + 1) * GM_GD
        u = _gelu(uv[:, lo:hi])
        v = _gelu(uv[:, GM_WIDTH + lo:GM_WIDTH + hi])
        mu = jnp.mean(v, axis=-1, keepdims=True)
        d = v - mu
        var = jnp.mean(d * d, axis=-1, keepdims=True)
        vn = (d * lax.rsqrt(var + EPS)) * lng_ref[:, lo:hi] + lnb_ref[:, lo:hi]
        vn_ref[:, lo:hi] = vn
        vnb = vn.astype(bf16)
        for c in range(tm // CHUNK):
            r0, r1 = c * CHUNK, (c + 1) * CHUNK
            mixed = _dot(wmix_ref[g], vnb[r0:r1]) + bmix_ref[g]
            a_ref[r0:r1, lo:hi] = (u[r0:r1] * mixed).astype(a_ref.dtype)
    q_ref[...] = _dot(h, wq_ref[...]).astype(q_ref.dtype)
    kv = _dot(h, wkv_ref[...])
    kv_ref[...] = kv
    kvb_ref[...] = kv.astype(bf16)
    gl = _dot(h, wgl_ref[...])
    gate_ref[...] = 1.0 / (1.0 + jnp.exp(-gl))


def _project(x, g1, wuv, wq, wkv, wgl, lng, lnb, wmix, bmix, q_dtype):
    n = x.shape[0]
    tm = PROJ_TM
    assert n % tm == 0
    row = lambda w: pl.BlockSpec((tm, w), lambda i: (i, 0))
    full = lambda a: pl.BlockSpec(a.shape, lambda i: (0,) * a.ndim)
    kvw = N_KV_BRANCH * KVW
    return pl.pallas_call(
        _project_kernel,
        grid=(n // tm,),
        in_specs=[row(D_MODEL), full(g1), full(wuv), full(wq), full(wkv), full(wgl),
                  full(lng), full(lnb), full(wmix), full(bmix)],
        out_specs=[row(GM_WIDTH), row(GM_WIDTH), row(NSA_HEADS * LANES), row(kvw), row(kvw),
                   row(KV_GROUPS * LANES)],
        out_shape=[jax.ShapeDtypeStruct((n, GM_WIDTH), bf16),
                   jax.ShapeDtypeStruct((n, GM_WIDTH), f32),
                   jax.ShapeDtypeStruct((n, NSA_HEADS * LANES), q_dtype),
                   jax.ShapeDtypeStruct((n, kvw), f32),
                   jax.ShapeDtypeStruct((n, kvw), bf16),
                   jax.ShapeDtypeStruct((n, KV_GROUPS * LANES), f32)],
        compiler_params=pltpu.CompilerParams(dimension_semantics=("arbitrary",),
                                             vmem_limit_bytes=VMEM_LIMIT),
        name="project",
    )(x, g1, wuv, wq, wkv, wgl, lng, lnb, wmix, bmix)


def _cmp_hidden_rows(load_rows, w4_ref, kind):
    accs = [None] * KV_GROUPS
    first_half = None
    for s4 in range(CMP_STRIDE // 4):
        pairs = []
        for s in (4 * s4, 4 * s4 + 2):
            a, b = load_rows(s), load_rows(s + 1)
            if first_half is None:
                first_half = lax.broadcasted_iota(jnp.int32, a.shape, 1) < HEAD_DIM
            pairs.append((jnp.where(first_half, a, pltpu.roll(b, HEAD_DIM, axis=1)),
                          jnp.where(first_half, pltpu.roll(a, HEAD_DIM, axis=1), b)))
        for g in range(KV_GROUPS):
            lhs = jnp.concatenate([pairs[0][g], pairs[1][g]], axis=1).astype(bf16)
            d = _dot(lhs, w4_ref[kind, s4])
            accs[g] = d if accs[g] is None else accs[g] + d
    return accs


def _cmp_hidden_kernel(xk_ref, xv_ref, wz_ref, h_ref):
    segs = h_ref.shape[0]
    hw = CMP_RATIO * CMP_HIDDEN
    for kind, x_ref in enumerate((xk_ref, xv_ref)):
        load = lambda s, x_ref=x_ref: x_ref[pl.ds(s, segs, stride=CMP_STRIDE), :]
        accs = _cmp_hidden_rows(load, wz_ref, kind)
        for g in range(KV_GROUPS):
            c0 = (kind * KV_GROUPS + g) * hw
            h_ref[:, c0:c0 + hw] = accs[g]


def _cmp_hidden(kv, wz):
    n = kv.shape[0]
    rows = CMP_SEGS * CMP_STRIDE
    assert n % rows == 0
    hw = 2 * KV_GROUPS * CMP_RATIO * CMP_HIDDEN
    return pl.pallas_call(
        _cmp_hidden_kernel,
        grid=(n // rows,),
        in_specs=[pl.BlockSpec((rows, LANES), lambda i: (i, 0)),
                  pl.BlockSpec((rows, LANES), lambda i: (i, 1)),
                  pl.BlockSpec(wz.shape, lambda i: (0,) * wz.ndim)],
        out_specs=pl.BlockSpec((CMP_SEGS, hw), lambda i: (i, 0)),
        out_shape=jax.ShapeDtypeStruct((n // CMP_STRIDE, hw), f32),
        compiler_params=pltpu.CompilerParams(dimension_semantics=("arbitrary",),
                                             vmem_limit_bytes=VMEM_LIMIT),
        name="cmp_hidden",
    )(kv, kv, wz)


def _cmp_bias_kernel(pe_ref, w1_ref, o_ref):
    for kind in range(2):
        o_ref[kind] = _dot(pe_ref[kind].astype(bf16), w1_ref[kind].astype(bf16))


def _cmp_bias(pe8, w1flat):
    return pl.pallas_call(
        _cmp_bias_kernel,
        out_shape=jax.ShapeDtypeStruct((2, SUBLANES, CMP_HIDDEN), f32),
        name="cmp_bias",
    )(pe8, w1flat)


def _cmp_finalize(hfull, bias_ref, w2p_ref):
    nseg = hfull.shape[0]
    hw = CMP_RATIO * CMP_HIDDEN
    outs = []
    for kind in range(2):
        acc = None
        for g in range(KV_GROUPS):
            c0 = (kind * KV_GROUPS + g) * hw
            h0 = hfull[:, c0:c0 + CMP_HIDDEN]
            h1 = hfull[:, c0 + CMP_HIDDEN:c0 + hw]
            hid = h0 + pltpu.roll(h1, nseg - 1, axis=0) + bias_ref[kind, 0:1, :]
            d = _dot(_gelu(hid).astype(bf16), w2p_ref[kind, g])
            acc = d if acc is None else acc + d
        outs.append(acc)
    return outs


def _cmp_final_kernel(h_ref, bias_ref, w2p_ref, kc_ref, vc_ref):
    kc, vc = _cmp_finalize(h_ref, bias_ref, w2p_ref)
    kc_ref[...] = kc.astype(bf16)
    vc_ref[...] = vc.astype(bf16)


def _cmp_final(hid, bias, w2p, batch):
    nseg = hid.shape[0] // batch
    hw = hid.shape[1]
    full = lambda a: pl.BlockSpec(a.shape, lambda b: (0,) * a.ndim)
    out = jax.ShapeDtypeStruct((batch, nseg, LANES), bf16)
    return pl.pallas_call(
        _cmp_final_kernel,
        grid=(batch,),
        in_specs=[pl.BlockSpec((nseg, hw), lambda b: (b, 0)), full(bias), full(w2p)],
        out_specs=[pl.BlockSpec((None, nseg, LANES), lambda b: (b, 0, 0))] * 2,
        out_shape=[out, out],
        compiler_params=pltpu.CompilerParams(dimension_semantics=("arbitrary",),
                                             vmem_limit_bytes=VMEM_LIMIT),
        name="cmp_final",
    )(hid, bias, w2p)


def _softmax_rows(s, mask):
    s = jnp.where(mask, s, NEG)
    m = jnp.max(s, axis=-1, keepdims=True)
    p = jnp.where(mask, jnp.exp(s - m), 0.0)
    l = jnp.sum(p, axis=-1, keepdims=True)
    return p, 1.0 / jnp.maximum(l, 1e-30)


def _importance(psum, m_ref):
    hi = psum.astype(bf16)
    lo = (psum - hi.astype(f32)).astype(bf16)
    return _dot(hi, m_ref[...]) + _dot(lo, m_ref[...])


def _select_bias(score, rounds):
    st = score.T
    blk = lax.broadcasted_iota(jnp.int32, st.shape, 0).astype(f32)

    def body(_, carry):
        s, bias = carry
        m = jnp.max(s, axis=0, keepdims=True)
        idx = jnp.min(jnp.where(s == m, blk, float(st.shape[0])), axis=0, keepdims=True)
        pick = blk == idx
        return jnp.where(pick, -jnp.inf, s), jnp.where(pick, 0.0, bias)

    _, bias = lax.fori_loop(0, rounds, body, (st, jnp.full(st.shape, NEG, f32)))
    return bias.T


def _with_ones(v):
    return jnp.concatenate([v, jnp.ones(v.shape, v.dtype)], axis=1)


def _attn_prompt_kernel(q_ref, gate_ref, kc_ref, vc_ref, ks_ref, vs_ref, kw_ref, vw_ref,
                        m_ref, e_ref, o_ref, acc_sc, m_sc, s0_sc, s1_sc, p_sc, *, nc):
    i = pl.program_id(2)
    qb = Q_BLOCK
    q0 = i * qb
    head = lambda h: slice(h * qb, (h + 1) * qb)
    q = jnp.concatenate([q_ref[:, h * LANES:(h + 1) * LANES] for h in range(HPG)], axis=0)
    qpos = q0 + lax.broadcasted_iota(jnp.int32, (qb, 1), 0)

    kc = kc_ref[...]
    nseg = kc.shape[0]
    n_idx = lax.broadcasted_iota(jnp.int32, (qb, nseg), 1)
    cmask = (n_idx * CMP_STRIDE + (CMP_LEN - 1) <= qpos) & (n_idx < nc)
    s_c = _dot_nt(q, kc)
    p_c = []
    psum = None
    for h in range(HPG):
        p, inv = _softmax_rows(s_c[head(h)], cmask)
        p = p * inv
        p_c.append(p.astype(bf16))
        psum = p if psum is None else psum + p
    o_c = _dot(jnp.concatenate(p_c, axis=0), vc_ref[...])

    imp = _importance(psum, m_ref)
    n_slc = imp.shape[1]
    blk = lax.broadcasted_iota(jnp.int32, (qb, n_slc), 1)
    cur = jnp.right_shift(q0 + lax.broadcasted_iota(jnp.int32, (qb, n_slc), 0), SLC_SHIFT)
    forced = (blk == 0) | (blk == cur) | (blk == cur - 1)
    score = jnp.where(forced, FORCE_SCORE, jnp.where(blk <= cur, imp, -jnp.inf))
    sel_bias = _select_bias(score, min(TOP_N, n_slc)).astype(bf16)

    q_aug = jnp.concatenate([q, jnp.concatenate([sel_bias] * HPG, axis=0)], axis=1)
    acc_sc[...] = jnp.zeros(acc_sc.shape, f32)
    m_sc[...] = jnp.full(m_sc.shape, NEG, f32)
    kc_len = SEL_KC

    s_bufs = (s0_sc, s1_sc)

    def scores(c, buf):
        k0 = pl.multiple_of(c * kc_len, kc_len)
        k_aug = jnp.concatenate([ks_ref[pl.ds(k0, kc_len), :], e_ref[pl.ds(k0, kc_len), :]], axis=1)
        s_bufs[buf][...] = _dot_nt(q_aug, k_aug)

    def consume(c, buf, causal):
        k0 = pl.multiple_of(c * kc_len, kc_len)
        if causal:
            ok = k0 + lax.broadcasted_iota(jnp.int32, (qb, kc_len), 1) <= qpos
        alphas = []
        for h in range(HPG):
            s = s_bufs[buf][head(h), :]
            if causal:
                s = jnp.where(ok, s, NEG)
            m_old = m_sc[head(h), :]
            m_new = jnp.maximum(m_old, jnp.max(s, axis=-1, keepdims=True))
            p_sc[head(h), :] = jnp.exp(s - m_new).astype(bf16)
            alphas.append(jnp.exp(m_old - m_new))
            m_sc[head(h), :] = m_new
        pv = _dot(p_sc[...], _with_ones(vs_ref[pl.ds(k0, kc_len), :]))
        for h in range(HPG):
            acc_sc[head(h), :] = alphas[h] * acc_sc[head(h), :] + pv[head(h)]

    def body(k, carry):
        c = 2 * k
        scores(c + 1, 1)
        consume(c, 0, False)
        scores(c + 2, 0)
        consume(c + 1, 1, False)
        return carry

    c_diag = q0 // kc_len
    scores(0, 0)
    lax.fori_loop(0, c_diag // 2, body, 0)

    @pl.when(c_diag % 2 == 0)
    def _():
        consume(c_diag, 0, True)

    @pl.when(c_diag % 2 == 1)
    def _():
        scores(c_diag, 1)
        consume(c_diag - 1, 0, False)
        consume(c_diag, 1, True)

    wlen = WINDOW + qb
    w0 = pl.multiple_of(jnp.maximum(q0 - WINDOW, 0), qb)
    dist = qpos - (w0 + lax.broadcasted_iota(jnp.int32, (qb, wlen), 1))
    wmask = (dist >= 0) & (dist < WINDOW)
    s_w = _dot_nt(q, kw_ref[pl.ds(w0, wlen), :])
    p_w = []
    for h in range(HPG):
        s = jnp.where(wmask, s_w[head(h)], NEG)
        p_w.append(jnp.exp(s - jnp.max(s, axis=-1, keepdims=True)).astype(bf16))
    o_w = _dot(jnp.concatenate(p_w, axis=0), _with_ones(vw_ref[pl.ds(w0, wlen), :]))

    gt = gate_ref[...]
    for h in range(HPG):
        ow = o_w[head(h)]
        acc = acc_sc[head(h), :]
        c = h * N_GATES
        o = (o_c[head(h)] * gt[:, c:c + 1]
             + acc[:, :LANES] * (1.0 / acc[:, LANES:]) * gt[:, c + 1:c + 2]
             + ow[:, :LANES] * (1.0 / ow[:, LANES:]) * gt[:, c + 2:c + 3])
        o_ref[:, h * LANES:(h + 1) * LANES] = o.astype(o_ref.dtype)


def _attn_prompt(qpad, gates, kc, vc, kvb, imp_m, sel_e, batch):
    n = qpad.shape[0]
    t = n // batch
    qb = Q_BLOCK
    nqb = t // qb
    assert t % SEL_KC == 0 and t >= WINDOW + qb
    nseg = t // CMP_STRIDE
    nc = nseg - CMP_RATIO + 1
    gw = HPG * LANES
    col = lambda c: pl.BlockSpec((t, LANES), lambda b, g, i, c=c: (b, c))
    cmp_spec = pl.BlockSpec((None, nseg, LANES), lambda b, g, i: (b, 0, 0))
    full = lambda a: pl.BlockSpec(a.shape, lambda b, g, i: (0,) * a.ndim)
    return pl.pallas_call(
        functools.partial(_attn_prompt_kernel, nc=nc),
        grid=(batch, KV_GROUPS, nqb),
        in_specs=[pl.BlockSpec((qb, gw), lambda b, g, i: (b * nqb + i, g)),
                  pl.BlockSpec((qb, LANES), lambda b, g, i: (b * nqb + i, g)),
                  cmp_spec, cmp_spec, col(2), col(3), col(4), col(5),
                  full(imp_m), full(sel_e)],
        out_specs=pl.BlockSpec((qb, gw), lambda b, g, i: (b * nqb + i, g)),
        out_shape=jax.ShapeDtypeStruct((n, NSA_HEADS * LANES), bf16),
        scratch_shapes=[pltpu.VMEM((HPG * qb, 2 * LANES), f32), pltpu.VMEM((HPG * qb, 1), f32),
                        pltpu.VMEM((HPG * qb, SEL_KC), f32), pltpu.VMEM((HPG * qb, SEL_KC), f32),
                        pltpu.VMEM((HPG * qb, SEL_KC), bf16)],
        compiler_params=pltpu.CompilerParams(
            dimension_semantics=("arbitrary", "arbitrary", "arbitrary"),
            vmem_limit_bytes=VMEM_LIMIT),
        name="attn_prompt",
    )(qpad, gates, kc, vc, kvb, kvb, kvb, kvb, imp_m, sel_e)


def _attn_sample_kernel(pt_ref, *refs, past, tdec):
    del pt_ref
    npg = PAGES_PER_STEP
    pages = refs[:npg]
    (q_ref, gate_ref, kv_ref, win_ref, perm_ref, wz_ref, bias_ref, w2p_ref, m_ref, e_ref,
     o_ref, xs_sc, h_sc, ks_sc, vs_sc) = refs[npg:]
    j = pl.program_id(1)
    nj = pl.num_programs(1)
    segs_pp = PAGE_SIZE // CMP_STRIDE
    segs = npg * segs_pp
    hw = CMP_RATIO * CMP_HIDDEN

    @pl.when((pl.program_id(0) == 0) & (j == 0))
    def _():
        ks_sc[LANES:2 * LANES, :] = e_ref[...]

    perm = perm_ref[...]
    for pi, pg in enumerate(pages):
        xp = _dot_nt(perm, pg[0:2 * LANES, :].astype(bf16))
        for s in range(CMP_STRIDE):
            rows_s = xp[s * segs_pp:(s + 1) * segs_pp]
            for kind in range(2):
                xs_sc[kind, s, pi * segs_pp:(pi + 1) * segs_pp, :] = (
                    rows_s[:, kind * LANES:(kind + 1) * LANES])
        c0 = pl.multiple_of((j * npg + pi) * PAGE_SIZE, PAGE_SIZE)
        ks_sc[0:LANES, pl.ds(c0, PAGE_SIZE)] = pg[2 * LANES:3 * LANES, :].astype(bf16)
        vs_sc[:, pl.ds(c0, PAGE_SIZE)] = pg[3 * LANES:4 * LANES, :].astype(bf16)

    seg0 = pl.multiple_of(j * segs, segs)
    for kind in range(2):
        load = lambda s, kind=kind: xs_sc[kind, s]
        accs = _cmp_hidden_rows(load, wz_ref, kind)
        for g in range(KV_GROUPS):
            c0 = (kind * KV_GROUPS + g) * hw
            h_sc[pl.ds(seg0, segs), c0:c0 + hw] = accs[g]

    @pl.when(j == nj - 1)
    def _():
        nh = NSA_HEADS
        rows = nh * tdec
        nseg = past // CMP_STRIDE
        nc = (past + tdec) // CMP_STRIDE - CMP_RATIO + 1
        kc, vc = _cmp_finalize(h_sc, bias_ref, w2p_ref)
        qf = q_ref[...]
        q = jnp.concatenate([qf[:, h * LANES:(h + 1) * LANES] for h in range(nh)],
                            axis=0).astype(bf16)
        trow = lax.broadcasted_iota(jnp.int32, (rows, 1), 0) & (tdec - 1)
        qpos = past + trow

        n_idx = lax.broadcasted_iota(jnp.int32, (rows, nseg), 1)
        cmask = (n_idx * CMP_STRIDE + (CMP_LEN - 1) <= qpos) & (n_idx < nc)
        p, inv = _softmax_rows(_dot_nt(q, kc.astype(bf16)), cmask)
        p = p * inv
        o_c = _dot(p.astype(bf16), vc.astype(bf16))

        gr = HPG * tdec
        psums = []
        for g in range(KV_GROUPS):
            ps = p[g * gr:g * gr + tdec]
            for h in range(1, HPG):
                ps = ps + p[g * gr + h * tdec:g * gr + (h + 1) * tdec]
            psums.append(ps)
        imp = _importance(jnp.concatenate(psums, axis=0), m_ref)
        n_past_blk = past // SLC_BLOCK
        blk = lax.broadcasted_iota(jnp.int32, imp.shape, 1)
        forced = (blk == 0) | (blk == n_past_blk - 1)
        score = jnp.concatenate([jnp.where(forced, FORCE_SCORE, imp),
                                 jnp.zeros((LANES - imp.shape[0], imp.shape[1]), f32)], axis=0)
        sel_bias = _select_bias(score, TOP_N - 1)
        q_aug = jnp.concatenate(
            [q, jnp.concatenate([sel_bias[g * tdec:(g + 1) * tdec] for g in range(KV_GROUPS)
                                 for _ in range(HPG)], axis=0).astype(bf16)], axis=1)

        def new_rows(c):
            blk_new = kv_ref[:, c * LANES:(c + 1) * LANES].astype(bf16)
            return jnp.concatenate([blk_new, jnp.zeros((LANES - tdec, LANES), bf16)], axis=0)

        new_mask = lax.broadcasted_iota(jnp.int32, (rows, LANES), 1) <= trow

        def two_part(s_old, v_old_t, c_k, c_v):
            s_new = jnp.where(new_mask, _dot_nt(q, new_rows(c_k)), NEG)
            m = jnp.maximum(jnp.max(s_old, axis=-1, keepdims=True),
                            jnp.max(s_new, axis=-1, keepdims=True))
            p_old = jnp.exp(s_old - m)
            p_new = jnp.exp(s_new - m)
            l = jnp.sum(p_old, axis=-1, keepdims=True) + jnp.sum(p_new, axis=-1, keepdims=True)
            o = _dot_nt(p_old.astype(bf16), v_old_t) + _dot(p_new.astype(bf16), new_rows(c_v))
            return o * (1.0 / l)

        o_s = two_part(_dot(q_aug, ks_sc[...]), vs_sc[...], 2, 3)

        wb = win_ref.shape[1]
        dist = wb + trow - lax.broadcasted_iota(jnp.int32, (rows, wb), 1)
        s_w = _dot(q, win_ref[0:LANES, :].astype(bf16))
        s_w = jnp.where((dist >= 0) & (dist < WINDOW), s_w, NEG)
        o_w = two_part(s_w, win_ref[LANES:2 * LANES, :].astype(bf16), 4, 5)

        gt = gate_ref[...]
        for h in range(nh):
            r0, r1 = h * tdec, (h + 1) * tdec
            c = (h // HPG) * LANES + (h % HPG) * N_GATES
            o = (o_c[r0:r1] * gt[:, c:c + 1] + o_s[r0:r1] * gt[:, c + 1:c + 2]
                 + o_w[r0:r1] * gt[:, c + 2:c + 3])
            o_ref[:, h * LANES:(h + 1) * LANES] = o.astype(o_ref.dtype)


def _attn_sample(page_table, cache, q3, gates3, kv3, win, perm, wz, bias, w2p, imp_m, sel_e):
    bd, n_pages = page_table.shape
    tdec = q3.shape[1]
    past = n_pages * PAGE_SIZE
    npg = PAGES_PER_STEP
    assert n_pages % npg == 0 and past % SLC_BLOCK == 0 and tdec <= CMP_STRIDE
    assert past // SLC_BLOCK == LANES and win.shape[2] <= WINDOW
    nj = n_pages // npg
    nseg = past // CMP_STRIDE
    hw = 2 * KV_GROUPS * CMP_RATIO * CMP_HIDDEN
    page_specs = [
        pl.BlockSpec((None, 4 * KVW, PAGE_SIZE),
                     lambda b, j, pt, p=p: (pt[b * n_pages + j * npg + p], 0, 0))
        for p in range(npg)]
    per_b = lambda a: pl.BlockSpec((None,) + a.shape[1:], lambda b, j, pt: (b,) + (0,) * (a.ndim - 1))
    full = lambda a: pl.BlockSpec(a.shape, lambda b, j, pt: (0,) * a.ndim)
    grid_spec = pltpu.PrefetchScalarGridSpec(
        num_scalar_prefetch=1,
        grid=(bd, nj),
        in_specs=page_specs + [per_b(q3), per_b(gates3), per_b(kv3), per_b(win), full(perm),
                               full(wz), full(bias), full(w2p), full(imp_m), full(sel_e)],
        out_specs=pl.BlockSpec((None, tdec, NSA_HEADS * LANES), lambda b, j, pt: (b, 0, 0)),
        scratch_shapes=[pltpu.VMEM((2, CMP_STRIDE, npg * PAGE_SIZE // CMP_STRIDE, LANES), f32),
                        pltpu.VMEM((nseg, hw), f32), pltpu.VMEM((2 * LANES, past), bf16),
                        pltpu.VMEM((LANES, past), bf16)],
    )
    return pl.pallas_call(
        functools.partial(_attn_sample_kernel, past=past, tdec=tdec),
        grid_spec=grid_spec,
        out_shape=jax.ShapeDtypeStruct((bd, tdec, NSA_HEADS * LANES), f32),
        compiler_params=pltpu.CompilerParams(dimension_semantics=("arbitrary", "arbitrary"),
                                             vmem_limit_bytes=VMEM_LIMIT),
        name="attn_sample",
    )(page_table.reshape(-1), *([cache] * npg), q3, gates3, kv3, win, perm, wz, bias, w2p, imp_m,
      sel_e)


def _ffn_kernel(*refs, tiles_per_seq, tdec):
    if tdec:
        (x_ref, a_ref, o_ref, wout_ref, g2_ref, gf_ref, wg_ref, wv_ref, cwg_ref, cwv_ref,
         cbg_ref, cbv_ref, wd_ref, pg_ref, pv_ref, y_ref, tg_ref, tv_ref,
         hp_sc, h2_sc, acc_sc) = refs
    else:
        (x_ref, a_ref, o_ref, wout_ref, g2_ref, gf_ref, wg_ref, wv_ref, cwg_ref, cwv_ref,
         cbg_ref, cbv_ref, wd_ref, y_ref, tg_ref, tv_ref,
         hp_sc, h2_sc, acc_sc, cg_sc, cv_sc) = refs
    i = pl.program_id(0)
    j = pl.program_id(1)
    nj = pl.num_programs(1)
    tm = x_ref.shape[0]

    @pl.when(j == 0)
    def _():
        cat = jnp.concatenate([a_ref[...], o_ref[...].astype(bf16)], axis=1)
        hp = x_ref[...] + _dot(cat, wout_ref[...])
        hp_sc[...] = hp
        h2_sc[...] = ((hp * _rms_scale(hp)) * g2_ref[...]).astype(bf16)
        acc_sc[...] = jnp.zeros(acc_sc.shape, f32)

    h2 = h2_sc[...]
    halves = []
    for w_ref, cw_ref, cb_ref, t_ref, extra in (
            (wg_ref, cwg_ref, cbg_ref, tg_ref, pg_ref if tdec else cg_sc),
            (wv_ref, cwv_ref, cbv_ref, tv_ref, pv_ref if tdec else cv_sc)):
        a = _dot(h2, w_ref[...])
        fc = a.shape[1]
        if tdec:
            nb = tm // tdec
            a3 = a.reshape(nb, tdec, fc)
            t_ref[...] = a3[:, tdec - (CONV_W - 1):, :]
            prev = extra[...]
            trow = lax.broadcasted_iota(jnp.int32, (nb, tdec, fc), 1)
            s1 = jnp.where(trow == 0, prev[:, 1:2, :], pltpu.roll(a3, 1, axis=1))
            s2 = jnp.where(trow == 0, prev[:, 0:1, :],
                           jnp.where(trow == 1, prev[:, 1:2, :], pltpu.roll(a3, 2, axis=1)))
            c = (s2 * cw_ref[0:1, :] + s1 * cw_ref[1:2, :] + a3 * cw_ref[2:3, :]
                 + cb_ref[...]).reshape(tm, fc)
        else:
            tail = a[tm - SUBLANES:, :]
            t_ref[...] = tail

            @pl.when(i % tiles_per_seq == 0)
            def _(extra=extra):
                extra[j] = jnp.zeros(extra.shape[1:], f32)

            prev = extra[j]
            row = lax.broadcasted_iota(jnp.int32, (tm, fc), 0)
            s1 = jnp.where(row == 0, prev[SUBLANES - 1:SUBLANES, :], pltpu.roll(a, 1, axis=0))
            s2 = jnp.where(row == 0, prev[SUBLANES - 2:SUBLANES - 1, :],
                           jnp.where(row == 1, prev[SUBLANES - 1:SUBLANES, :],
                                     pltpu.roll(a, 2, axis=0)))
            extra[j] = tail
            c = s2 * cw_ref[0:1, :] + s1 * cw_ref[1:2, :] + a * cw_ref[2:3, :] + cb_ref[...]
        halves.append(c)
    act = (_gelu(halves[0]) * halves[1]).astype(bf16)
    acc_sc[...] += _dot(act, wd_ref[...])

    @pl.when(j == nj - 1)
    def _():
        out = hp_sc[...] + acc_sc[...]
        y_ref[...] = (out * _rms_scale(out)) * gf_ref[...]


def _ffn(x, a_out, o_pad, wout, g2, gf, wup, cw, cb, wdn, state, seq_len, tdec, tm):
    n = x.shape[0]
    fc = FFN_FC
    assert n % tm == 0 and D_FF % fc == 0
    ni, nj = n // tm, D_FF // fc
    row = lambda w: pl.BlockSpec((tm, w), lambda i, j: (i, 0))
    full = lambda a: pl.BlockSpec(a.shape, lambda i, j: (0,) * a.ndim)
    gate_col = lambda r: pl.BlockSpec((r, fc), lambda i, j: (0, j))
    val_col = lambda r: pl.BlockSpec((r, fc), lambda i, j: (0, nj + j))
    in_specs = [row(D_MODEL), row(GM_WIDTH), row(NSA_HEADS * LANES), full(wout), full(g2), full(gf),
                gate_col(D_MODEL), val_col(D_MODEL), gate_col(CONV_W), val_col(CONV_W),
                gate_col(1), val_col(1), pl.BlockSpec((fc, D_MODEL), lambda i, j: (j, 0))]
    args = [x, a_out, o_pad, wout, g2, gf, wup, wup, cw, cw, cb, cb, wdn]
    scratch = [pltpu.VMEM((tm, D_MODEL), f32), pltpu.VMEM((tm, D_MODEL), bf16),
               pltpu.VMEM((tm, D_MODEL), f32)]
    if tdec:
        assert tm % tdec == 0 and tdec == SUBLANES
        nb = tm // tdec
        in_specs += [pl.BlockSpec((nb, CONV_W - 1, fc), lambda i, j: (i, 0, j)),
                     pl.BlockSpec((nb, CONV_W - 1, fc), lambda i, j: (i, 0, nj + j))]
        args += [state, state]
        tail_shape = jax.ShapeDtypeStruct((n // tdec, CONV_W - 1, D_FF), f32)
        tail_spec = pl.BlockSpec((nb, CONV_W - 1, fc), lambda i, j: (i, 0, j))
        tiles_per_seq = 0
    else:
        assert seq_len % tm == 0
        tiles_per_seq = seq_len // tm
        tail_shape = jax.ShapeDtypeStruct((ni, SUBLANES, D_FF), f32)
        tail_spec = pl.BlockSpec((None, SUBLANES, fc), lambda i, j: (i, 0, j))
        scratch += [pltpu.VMEM((nj, SUBLANES, fc), f32), pltpu.VMEM((nj, SUBLANES, fc), f32)]
    return pl.pallas_call(
        functools.partial(_ffn_kernel, tiles_per_seq=tiles_per_seq, tdec=tdec),
        grid=(ni, nj),
        in_specs=in_specs,
        out_specs=[row(D_MODEL), tail_spec, tail_spec],
        out_shape=[jax.ShapeDtypeStruct((n, D_MODEL), f32), tail_shape, tail_shape],
        scratch_shapes=scratch,
        compiler_params=pltpu.CompilerParams(dimension_semantics=("arbitrary", "arbitrary"),
                                             vmem_limit_bytes=VMEM_LIMIT),
        name="ffn_sample" if tdec else "ffn_prompt",
    )(*args)


def _pad_group_lanes(w, heads_axis_len):
    lead = w.shape[:-1]
    w = w.reshape(lead + (heads_axis_len, HEAD_DIM))
    z = jnp.zeros_like(w)
    first = (jnp.arange(heads_axis_len) < HPG)[:, None]
    lo = jnp.where(first, w, z)
    hi = jnp.where(first, z, w)
    return jnp.concatenate([lo, hi], axis=-1).reshape(lead + (heads_axis_len * LANES,))


def _importance_matrix(nseg, n_blk):
    m = np.zeros((nseg, n_blk), np.float32)
    for n in range(nseg - CMP_RATIO + 1):
        for r in range(CMP_RATIO):
            b = (n + r) // SEG_PER_SLC
            if b < n_blk:
                m[n, b] += 1.0
    return jnp.asarray(m, bf16)


def _block_one_hot(n_blk, n_keys):
    e = (np.arange(n_keys)[None, :] // SLC_BLOCK) == np.arange(n_blk)[:, None]
    return e.astype(np.float32)


def _segment_permutation():
    segs = PAGE_SIZE // CMP_STRIDE
    p = np.zeros((PAGE_SIZE, PAGE_SIZE), np.float32)
    for s in range(CMP_STRIDE):
        for n in range(segs):
            p[s * segs + n, n * CMP_STRIDE + s] = 1.0
    return jnp.asarray(p, bf16)


def kernel(x_prompt, x_sample, cache_kv, cache_win, state_conv, page_table, norm1_g, w_in, gm_ln_g,
           gm_ln_b, gm_ws, gm_bs, cmp_pe, cmp_w1, cmp_w2, w_out, norm2_g, w_up, conv_w, conv_b,
           w_down, final_g):
    depth = w_in.shape[0]
    assert depth == 1
    l = 0
    B, T, _ = x_prompt.shape
    Bd, Td, _ = x_sample.shape
    assert T % CHUNK == 0 and Td == SUBLANES and CHUNK % Td == 0

    wi = w_in[l]
    o_q = 2 * GM_WIDTH
    o_kv = o_q + NSA_WIDTH
    o_gl = o_kv + N_KV_BRANCH * KVW
    wuv = wi[:, :o_q].astype(bf16)
    wq = _pad_group_lanes(wi[:, o_q:o_kv] * (HEAD_DIM ** -0.5), NSA_HEADS).astype(bf16)
    wkv = wi[:, o_kv:o_gl].astype(bf16)
    gpg = HPG * N_GATES
    wgl = jnp.concatenate(
        [jnp.pad(wi[:, o_gl + g * gpg:o_gl + (g + 1) * gpg], ((0, 0), (0, LANES - gpg)))
         for g in range(KV_GROUPS)], axis=1).astype(bf16)
    g1 = norm1_g[l].reshape(1, D_MODEL)
    lng = gm_ln_g[l].reshape(1, GM_WIDTH)
    lnb = gm_ln_b[l].reshape(1, GM_WIDTH)
    causal = jnp.tril(jnp.ones((CHUNK, CHUNK), bool))
    wmix_p = jnp.where(causal, gm_ws[l], 0.0).astype(bf16)
    bmix_p = jnp.broadcast_to(gm_bs[l][:, :, None], (GM_GROUPS, CHUNK, GM_GD))
    reps = CHUNK // Td
    ws_d = jnp.where(causal[:Td, :Td], gm_ws[l][:, :Td, :Td], 0.0)
    eye = jnp.eye(reps, dtype=f32)
    wmix_s = (eye[None, :, None, :, None] * ws_d[:, None, :, None, :]).reshape(
        GM_GROUPS, CHUNK, CHUNK).astype(bf16)
    bmix_s = jnp.broadcast_to(jnp.tile(gm_bs[l][:, :Td], (1, reps))[:, :, None],
                              (GM_GROUPS, CHUNK, GM_GD))

    w1r = cmp_w1[l].reshape(2, CMP_RATIO, CMP_STRIDE, HEAD_DIM, CMP_HIDDEN)
    w1cat = jnp.concatenate([w1r[:, r] for r in range(CMP_RATIO)], axis=-1)
    wz = w1cat.reshape(2, CMP_STRIDE // 4, 4 * HEAD_DIM, CMP_RATIO * CMP_HIDDEN).astype(bf16)
    pe8 = jnp.broadcast_to(cmp_pe[l].reshape(2, 1, CMP_LEN * HEAD_DIM), (2, SUBLANES, CMP_LEN * HEAD_DIM))
    cbias = _cmp_bias(pe8, cmp_w1[l].reshape(2, CMP_LEN * HEAD_DIM, CMP_HIDDEN))
    w2 = cmp_w2[l]
    z2 = jnp.zeros_like(w2)
    w2p = jnp.stack([jnp.concatenate([w2, z2], axis=-1),
                     jnp.concatenate([z2, w2], axis=-1)], axis=1).astype(bf16)

    wo = w_out[l]
    wout = jnp.concatenate([wo[:GM_WIDTH], _pad_group_lanes(wo[GM_WIDTH:].T, NSA_HEADS).T],
                           axis=0).astype(bf16)
    g2 = norm2_g[l].reshape(1, D_MODEL)
    gf = final_g.reshape(1, D_MODEL)
    wup = w_up[l].astype(bf16)
    wdn = w_down[l].astype(bf16)
    cw = conv_w[l]
    cb = conv_b[l].reshape(1, 2 * D_FF)

    xp = x_prompt.reshape(B * T, D_MODEL)
    a_p, _, q_p, kv_p, kvb_p, gate_p = _project(xp, g1, wuv, wq, wkv, wgl, lng, lnb, wmix_p, bmix_p, bf16)
    hid_p = _cmp_hidden(kv_p, wz)
    kc_p, vc_p = _cmp_final(hid_p, cbias, w2p, B)
    imp_m = _importance_matrix(T // CMP_STRIDE, T // SLC_BLOCK)
    one_hot_t = jnp.asarray(_block_one_hot(T // SLC_BLOCK, T).T, bf16)
    o_p = _attn_prompt(q_p, gate_p, kc_p, vc_p, kvb_p, imp_m, one_hot_t, B)
    y_p, tg_p, tv_p = _ffn(xp, a_p, o_p, wout, g2, gf, wup, cw, cb, wdn, None, T, 0, FFN_TM)
    tiles = T // FFN_TM
    tail_p = jnp.concatenate([tg_p, tv_p], axis=-1).reshape(B, tiles, SUBLANES, 2 * D_FF)
    conv_prompt = tail_p[:, tiles - 1, SUBLANES - (CONV_W - 1):]
    kv6 = kv_p.reshape(B, T, N_KV_BRANCH, KV_GROUPS, HEAD_DIM)
    wp = min(WINDOW, T)

    n_pages = page_table.shape[1]
    past = n_pages * PAGE_SIZE
    xs = x_sample.reshape(Bd * Td, D_MODEL)
    a_s, vn_s, q_s, kv_s, _, gate_s = _project(xs, g1, wuv, wq, wkv, wgl, lng, lnb, wmix_s, bmix_s, f32)
    cache = jnp.transpose(cache_kv[l], (0, 2, 3, 4, 1)).reshape(cache_kv.shape[1], 4 * KVW, PAGE_SIZE)
    wb = cache_win.shape[2]
    win = jnp.transpose(cache_win[l], (0, 2, 3, 4, 1)).reshape(Bd, 2 * KVW, wb)
    o_s = _attn_sample(page_table, cache, q_s.reshape(Bd, Td, -1), gate_s.reshape(Bd, Td, -1),
                       kv_s.reshape(Bd, Td, -1), win, _segment_permutation(), wz, cbias, w2p,
                       _importance_matrix(past // CMP_STRIDE, past // SLC_BLOCK),
                       jnp.asarray(_block_one_hot(past // SLC_BLOCK, past), bf16))
    tm_s = min(FFN_TM, Bd * Td)
    y_s, tg_s, tv_s = _ffn(xs, a_s, o_s.reshape(Bd * Td, -1), wout, g2, gf, wup, cw, cb, wdn,
                           state_conv[l], Td, Td, tm_s)
    kv6_s = kv_s.reshape(Bd, Td, N_KV_BRANCH, KV_GROUPS, HEAD_DIM)
    win_new = jnp.concatenate([cache_win[l], kv6_s[:, :, 4:]], axis=1)[:, Td:]

    return (y_p.reshape(B, T, D_MODEL),
            y_s.reshape(Bd, Td, D_MODEL),
            kv6[None, :, :, :4],
            kv6_s[None, :, :, :4],
            kv6[None, :, T - wp:, 4:],
            win_new[None],
            conv_prompt[None],
            jnp.concatenate([tg_s, tv_s], axis=-1)[None],
            vn_s.reshape(1, Bd, Td, GM_WIDTH))
```

```python
import functools
import math

import numpy as np
import jax
import jax.numpy as jnp
from jax import lax
from jax.experimental import pallas as pl
from jax.experimental.pallas import tpu as pltpu

D_MODEL = 1024
GM_WIDTH = D_MODEL // 2
GM_GROUPS = 4
GM_GD = GM_WIDTH // GM_GROUPS
CHUNK = 128
HEAD_DIM = 64
NSA_WIDTH = D_MODEL - GM_WIDTH
NSA_HEADS = NSA_WIDTH // HEAD_DIM
KV_GROUPS = 2
HPG = NSA_HEADS // KV_GROUPS
KVW = KV_GROUPS * HEAD_DIM
N_KV_BRANCH = 6
N_GATES = 3
CMP_LEN = 32
CMP_STRIDE = 16
CMP_RATIO = CMP_LEN // CMP_STRIDE
CMP_HIDDEN = 256
SLC_BLOCK = 64
SLC_SHIFT = SLC_BLOCK.bit_length() - 1
SEG_PER_SLC = SLC_BLOCK // CMP_STRIDE
TOP_N = 16
WINDOW = 512
Q_BLOCK = 128
FORCE_SCORE = 1e9
D_FF = ((8 * D_MODEL // 3 + 127) // 128) * 128
CONV_W = 3
EPS = 1e-6
PAGE_SIZE = 128

LANES = 128
SUBLANES = 8
NEG = -1e30
VMEM_LIMIT = 56 * 1024 * 1024

PROJ_TM = 256
CMP_SEGS = 128
SEL_KC = 512
FFN_TM = 512
FFN_FC = 256
FFN_DOWN_GROUP = 4
PAGES_PER_STEP = 16

f32 = jnp.float32
bf16 = jnp.bfloat16

_NT = (((1,), (1,)), ((), ()))


def _gelu(x):
    c = math.sqrt(2.0 / math.pi)
    return x * (0.5 * (1.0 + jnp.tanh(c * (x + 0.044715 * (x * x * x)))))


def _dot(a, b):
    return jnp.dot(a, b, preferred_element_type=f32)


def _dot_nt(a, b):
    return lax.dot_general(a, b, _NT, preferred_element_type=f32)


def _rms_scale(x):
    return lax.rsqrt(jnp.mean(x * x, axis=-1, keepdims=True) + EPS)


def _project_kernel(x_ref, g1_ref, wuv_ref, wq_ref, wkv_ref, wgl_ref, lng_ref, lnb_ref,
                    wmix_ref, bmix_ref, a_ref, vn_ref, q_ref, kv_ref, kvb_ref, gate_ref):
    x = x_ref[...]
    h = ((x * _rms_scale(x)) * g1_ref[...]).astype(bf16)
    uv = _dot(h, wuv_ref[...])
    tm = x.shape[0]
    for g in range(GM_GROUPS):
        lo, hi = g * GM_GD, (g + 1) * GM_GD
        u = _gelu(uv[:, lo:hi])
        v = _gelu(uv[:, GM_WIDTH + lo:GM_WIDTH + hi])
        mu = jnp.mean(v, axis=-1, keepdims=True)
        d = v - mu
        var = jnp.mean(d * d, axis=-1, keepdims=True)
        vn = (d * lax.rsqrt(var + EPS)) * lng_ref[:, lo:hi] + lnb_ref[:, lo:hi]
        vn_ref[:, lo:hi] = vn
        vnb = vn.astype(bf16)
        for c in range(tm // CHUNK):
            r0, r1 = c * CHUNK, (c + 1) * CHUNK
            mixed = _dot(wmix_ref[g], vnb[r0:r1]) + bmix_ref[g]
            a_ref[r0:r1, lo:hi] = (u[r0:r1] * mixed).astype(a_ref.dtype)
    q_ref[...] = _dot(h, wq_ref[...]).astype(q_ref.dtype)
    kv = _dot(h, wkv_ref[...])
    kv_ref[...] = kv
    kvb_ref[...] = kv.astype(bf16)
    gl = _dot(h, wgl_ref[...])
    gate_ref[...] = 1.0 / (1.0 + jnp.exp(-gl))


def _project(x, g1, wuv, wq, wkv, wgl, lng, lnb, wmix, bmix, q_dtype):
    n = x.shape[0]
    tm = PROJ_TM
    assert n % tm == 0
    row = lambda w: pl.BlockSpec((tm, w), lambda i: (i, 0))
    full = lambda a: pl.BlockSpec(a.shape, lambda i: (0,) * a.ndim)
    kvw = N_KV_BRANCH * KVW
    return pl.pallas_call(
        _project_kernel,
        grid=(n // tm,),
        in_specs=[row(D_MODEL), full(g1), full(wuv), full(wq), full(wkv), full(wgl),
                  full(lng), full(lnb), full(wmix), full(bmix)],
        out_specs=[row(GM_WIDTH), row(GM_WIDTH), row(NSA_HEADS * LANES), row(kvw), row(kvw),
                   row(KV_GROUPS * LANES)],
        out_shape=[jax.ShapeDtypeStruct((n, GM_WIDTH), bf16),
                   jax.ShapeDtypeStruct((n, GM_WIDTH), f32),
                   jax.ShapeDtypeStruct((n, NSA_HEADS * LANES), q_dtype),
                   jax.ShapeDtypeStruct((n, kvw), f32),
                   jax.ShapeDtypeStruct((n, kvw), bf16),
                   jax.ShapeDtypeStruct((n, KV_GROUPS * LANES), f32)],
        compiler_params=pltpu.CompilerParams(dimension_semantics=("arbitrary",),
                                             vmem_limit_bytes=VMEM_LIMIT),
        name="project",
    )(x, g1, wuv, wq, wkv, wgl, lng, lnb, wmix, bmix)


def _cmp_hidden_rows(load_rows, w4_ref, kind):
    accs = [None] * KV_GROUPS
    first_half = None
    for s4 in range(CMP_STRIDE // 4):
        pairs = []
        for s in (4 * s4, 4 * s4 + 2):
            a, b = load_rows(s), load_rows(s + 1)
            if first_half is None:
                first_half = lax.broadcasted_iota(jnp.int32, a.shape, 1) < HEAD_DIM
            pairs.append((jnp.where(first_half, a, pltpu.roll(b, HEAD_DIM, axis=1)),
                          jnp.where(first_half, pltpu.roll(a, HEAD_DIM, axis=1), b)))
        for g in range(KV_GROUPS):
            lhs = jnp.concatenate([pairs[0][g], pairs[1][g]], axis=1).astype(bf16)
            d = _dot(lhs, w4_ref[kind, s4])
            accs[g] = d if accs[g] is None else accs[g] + d
    return accs


def _cmp_hidden_kernel(xk_ref, xv_ref, wz_ref, h_ref):
    segs = h_ref.shape[0]
    hw = CMP_RATIO * CMP_HIDDEN
    for kind, x_ref in enumerate((xk_ref, xv_ref)):
        load = lambda s, x_ref=x_ref: x_ref[pl.ds(s, segs, stride=CMP_STRIDE), :]
        accs = _cmp_hidden_rows(load, wz_ref, kind)
        for g in range(KV_GROUPS):
            c0 = (kind * KV_GROUPS + g) * hw
            h_ref[:, c0:c0 + hw] = accs[g]


def _cmp_hidden(kv, wz):
    n = kv.shape[0]
    rows = CMP_SEGS * CMP_STRIDE
    assert n % rows == 0
    hw = 2 * KV_GROUPS * CMP_RATIO * CMP_HIDDEN
    return pl.pallas_call(
        _cmp_hidden_kernel,
        grid=(n // rows,),
        in_specs=[pl.BlockSpec((rows, LANES), lambda i: (i, 0)),
                  pl.BlockSpec((rows, LANES), lambda i: (i, 1)),
                  pl.BlockSpec(wz.shape, lambda i: (0,) * wz.ndim)],
        out_specs=pl.BlockSpec((CMP_SEGS, hw), lambda i: (i, 0)),
        out_shape=jax.ShapeDtypeStruct((n // CMP_STRIDE, hw), f32),
        compiler_params=pltpu.CompilerParams(dimension_semantics=("arbitrary",),
                                             vmem_limit_bytes=VMEM_LIMIT),
        name="cmp_hidden",
    )(kv, kv, wz)


def _cmp_bias_kernel(pe_ref, w1_ref, o_ref):
    for kind in range(2):
        o_ref[kind] = _dot(pe_ref[kind].astype(bf16), w1_ref[kind].astype(bf16))


def _cmp_bias(pe8, w1flat):
    return pl.pallas_call(
        _cmp_bias_kernel,
        out_shape=jax.ShapeDtypeStruct((2, SUBLANES, CMP_HIDDEN), f32),
        name="cmp_bias",
    )(pe8, w1flat)


def _cmp_finalize(hfull, bias_ref, w2p_ref):
    nseg = hfull.shape[0]
    hw = CMP_RATIO * CMP_HIDDEN
    outs = []
    for kind in range(2):
        acc = None
        for g in range(KV_GROUPS):
            c0 = (kind * KV_GROUPS + g) * hw
            h0 = hfull[:, c0:c0 + CMP_HIDDEN]
            h1 = hfull[:, c0 + CMP_HIDDEN:c0 + hw]
            hid = h0 + pltpu.roll(h1, nseg - 1, axis=0) + bias_ref[kind, 0:1, :]
            d = _dot(_gelu(hid).astype(bf16), w2p_ref[kind, g])
            acc = d if acc is None else acc + d
        outs.append(acc)
    return outs


def _cmp_final_kernel(h_ref, bias_ref, w2p_ref, kc_ref, vc_ref):
    kc, vc = _cmp_finalize(h_ref, bias_ref, w2p_ref)
    kc_ref[...] = kc.astype(bf16)
    vc_ref[...] = vc.astype(bf16)


def _cmp_final(hid, bias, w2p, batch):
    nseg = hid.shape[0] // batch
    hw = hid.shape[1]
    full = lambda a: pl.BlockSpec(a.shape, lambda b: (0,) * a.ndim)
    out = jax.ShapeDtypeStruct((batch, nseg, LANES), bf16)
    return pl.pallas_call(
        _cmp_final_kernel,
        grid=(batch,),
        in_specs=[pl.BlockSpec((nseg, hw), lambda b: (b, 0)), full(bias), full(w2p)],
        out_specs=[pl.BlockSpec((None, nseg, LANES), lambda b: (b, 0, 0))] * 2,
        out_shape=[out, out],
        compiler_params=pltpu.CompilerParams(dimension_semantics=("arbitrary",),
                                             vmem_limit_bytes=VMEM_LIMIT),
        name="cmp_final",
    )(hid, bias, w2p)


def _softmax_rows(s, mask):
    s = jnp.where(mask, s, NEG)
    m = jnp.max(s, axis=-1, keepdims=True)
    p = jnp.where(mask, jnp.exp(s - m), 0.0)
    l = jnp.sum(p, axis=-1, keepdims=True)
    return p, 1.0 / jnp.maximum(l, 1e-30)


def _importance(psum, m_ref):
    hi = psum.astype(bf16)
    lo = (psum - hi.astype(f32)).astype(bf16)
    return _dot(hi, m_ref[...]) + _dot(lo, m_ref[...])


def _select_bias(score, rounds):
    st = score.T
    blk = lax.broadcasted_iota(jnp.int32, st.shape, 0).astype(f32)

    def body(_, carry):
        s, bias = carry
        m = jnp.max(s, axis=0, keepdims=True)
        idx = jnp.min(jnp.where(s == m, blk, float(st.shape[0])), axis=0, keepdims=True)
        pick = blk == idx
        return jnp.where(pick, -jnp.inf, s), jnp.where(pick, 0.0, bias)

    _, bias = lax.fori_loop(0, rounds, body, (st, jnp.full(st.shape, NEG, f32)), unroll=True)
    return bias.T


def _with_ones(v):
    return jnp.concatenate([v, jnp.ones(v.shape, v.dtype)], axis=1)


def _attn_prompt_kernel(q_ref, gate_ref, kc_ref, vc_ref, ks_ref, vs_ref, kw_ref, vw_ref,
                        m_ref, e_ref, o_ref, acc_sc, m_sc, s0_sc, s1_sc, p_sc, *, nc):
    i = pl.program_id(2)
    qb = Q_BLOCK
    q0 = i * qb
    head = lambda h: slice(h * qb, (h + 1) * qb)
    q = jnp.concatenate([q_ref[:, h * LANES:(h + 1) * LANES] for h in range(HPG)], axis=0)
    qpos = q0 + lax.broadcasted_iota(jnp.int32, (qb, 1), 0)

    kc = kc_ref[...]
    nseg = kc.shape[0]
    n_idx = lax.broadcasted_iota(jnp.int32, (qb, nseg), 1)
    cmask = (n_idx * CMP_STRIDE + (CMP_LEN - 1) <= qpos) & (n_idx < nc)
    s_c = _dot_nt(q, kc)
    p_c = []
    psum = None
    for h in range(HPG):
        p, inv = _softmax_rows(s_c[head(h)], cmask)
        p = p * inv
        p_c.append(p.astype(bf16))
        psum = p if psum is None else psum + p
    o_c = _dot(jnp.concatenate(p_c, axis=0), vc_ref[...])

    wlen = WINDOW + qb
    w0 = pl.multiple_of(jnp.maximum(q0 - WINDOW, 0), qb)
    dist = qpos - (w0 + lax.broadcasted_iota(jnp.int32, (qb, wlen), 1))
    wmask = (dist >= 0) & (dist < WINDOW)
    s_w = _dot_nt(q, kw_ref[pl.ds(w0, wlen), :])
    p_w = []
    for h in range(HPG):
        s = jnp.where(wmask, s_w[head(h)], NEG)
        p_w.append(jnp.exp(s - jnp.max(s, axis=-1, keepdims=True)).astype(bf16))
    o_w = _dot(jnp.concatenate(p_w, axis=0), _with_ones(vw_ref[pl.ds(w0, wlen), :]))

    imp = _importance(psum, m_ref)
    n_slc = imp.shape[1]
    blk = lax.broadcasted_iota(jnp.int32, (qb, n_slc), 1)
    cur = jnp.right_shift(q0 + lax.broadcasted_iota(jnp.int32, (qb, n_slc), 0), SLC_SHIFT)
    forced = (blk == 0) | (blk == cur) | (blk == cur - 1)
    score = jnp.where(forced, FORCE_SCORE, jnp.where(blk <= cur, imp, -jnp.inf))
    sel_bias = _select_bias(score, min(TOP_N, n_slc)).astype(bf16)

    q_aug = jnp.concatenate([q, jnp.concatenate([sel_bias] * HPG, axis=0)], axis=1)
    acc_sc[...] = jnp.zeros(acc_sc.shape, f32)
    m_sc[...] = jnp.full(m_sc.shape, NEG, f32)
    kc_len = SEL_KC

    s_bufs = (s0_sc, s1_sc)

    def scores(c, buf):
        k0 = pl.multiple_of(c * kc_len, kc_len)
        k_aug = jnp.concatenate([ks_ref[pl.ds(k0, kc_len), :], e_ref[pl.ds(k0, kc_len), :]], axis=1)
        s_bufs[buf][...] = _dot_nt(q_aug, k_aug)

    def consume(c, buf, causal):
        k0 = pl.multiple_of(c * kc_len, kc_len)
        if causal:
            ok = k0 + lax.broadcasted_iota(jnp.int32, (qb, kc_len), 1) <= qpos
        alphas = []
        for h in range(HPG):
            s = s_bufs[buf][head(h), :]
            if causal:
                s = jnp.where(ok, s, NEG)
            m_old = m_sc[head(h), :]
            m_new = jnp.maximum(m_old, jnp.max(s, axis=-1, keepdims=True))
            p_sc[head(h), :] = jnp.exp(s - m_new).astype(bf16)
            alphas.append(jnp.exp(m_old - m_new))
            m_sc[head(h), :] = m_new
        pv = _dot(p_sc[...], _with_ones(vs_ref[pl.ds(k0, kc_len), :]))
        for h in range(HPG):
            acc_sc[head(h), :] = alphas[h] * acc_sc[head(h), :] + pv[head(h)]

    def body(k, carry):
        c = 2 * k
        scores(c + 1, 1)
        consume(c, 0, False)
        scores(c + 2, 0)
        consume(c + 1, 1, False)
        return carry

    c_diag = q0 // kc_len
    scores(0, 0)
    lax.fori_loop(0, c_diag // 2, body, 0)

    @pl.when(c_diag % 2 == 0)
    def _():
        consume(c_diag, 0, True)

    @pl.when(c_diag % 2 == 1)
    def _():
        scores(c_diag, 1)
        consume(c_diag - 1, 0, False)
        consume(c_diag, 1, True)

    gt = gate_ref[...]
    for h in range(HPG):
        ow = o_w[head(h)]
        acc = acc_sc[head(h), :]
        c = h * N_GATES
        o = (o_c[head(h)] * gt[:, c:c + 1]
             + acc[:, :LANES] * (1.0 / acc[:, LANES:]) * gt[:, c + 1:c + 2]
             + ow[:, :LANES] * (1.0 / ow[:, LANES:]) * gt[:, c + 2:c + 3])
        o_ref[:, h * LANES:(h + 1) * LANES] = o.astype(o_ref.dtype)


def _attn_prompt(qpad, gates, kc, vc, kvb, imp_m, sel_e, batch):
    n = qpad.shape[0]
    t = n // batch
    qb = Q_BLOCK
    nqb = t // qb
    assert t % SEL_KC == 0 and t >= WINDOW + qb
    nseg = t // CMP_STRIDE
    nc = nseg - CMP_RATIO + 1
    gw = HPG * LANES
    col = lambda c: pl.BlockSpec((t, LANES), lambda b, g, i, c=c: (b, c))
    cmp_spec = pl.BlockSpec((None, nseg, LANES), lambda b, g, i: (b, 0, 0))
    full = lambda a: pl.BlockSpec(a.shape, lambda b, g, i: (0,) * a.ndim)
    return pl.pallas_call(
        functools.partial(_attn_prompt_kernel, nc=nc),
        grid=(batch, KV_GROUPS, nqb),
        in_specs=[pl.BlockSpec((qb, gw), lambda b, g, i: (b * nqb + i, g)),
                  pl.BlockSpec((qb, LANES), lambda b, g, i: (b * nqb + i, g)),
                  cmp_spec, cmp_spec, col(2), col(3), col(4), col(5),
                  full(imp_m), full(sel_e)],
        out_specs=pl.BlockSpec((qb, gw), lambda b, g, i: (b * nqb + i, g)),
        out_shape=jax.ShapeDtypeStruct((n, NSA_HEADS * LANES), bf16),
        scratch_shapes=[pltpu.VMEM((HPG * qb, 2 * LANES), f32), pltpu.VMEM((HPG * qb, 1), f32),
                        pltpu.VMEM((HPG * qb, SEL_KC), f32), pltpu.VMEM((HPG * qb, SEL_KC), f32),
                        pltpu.VMEM((HPG * qb, SEL_KC), bf16)],
        compiler_params=pltpu.CompilerParams(
            dimension_semantics=("arbitrary", "arbitrary", "arbitrary"),
            vmem_limit_bytes=VMEM_LIMIT),
        name="attn_prompt",
    )(qpad, gates, kc, vc, kvb, kvb, kvb, kvb, imp_m, sel_e)


def _attn_sample_kernel(pt_ref, *refs, past, tdec, steps):
    del pt_ref
    npg = PAGES_PER_STEP
    pages = refs[:npg]
    (q_ref, gate_ref, kv_ref, win_ref, perm_ref, wz_ref, bias_ref, w2p_ref, m_ref, e_ref,
     o_ref, xs0_sc, xs1_sc, h_sc, ks_sc, vs_sc) = refs[npg:]
    j = pl.program_id(1)
    nj = pl.num_programs(1)
    segs_pp = PAGE_SIZE // CMP_STRIDE
    segs = npg * segs_pp
    hw = CMP_RATIO * CMP_HIDDEN
    xs_bufs = (xs0_sc, xs1_sc)

    @pl.when((pl.program_id(0) == 0) & (j == 0))
    def _():
        ks_sc[LANES:2 * LANES, :] = e_ref[...]

    def unpack_pages(xs_sc):
        perm = perm_ref[...]
        for pi, pg in enumerate(pages):
            xp = _dot_nt(perm, pg[0:2 * LANES, :].astype(bf16))
            for s in range(CMP_STRIDE):
                rows_s = xp[s * segs_pp:(s + 1) * segs_pp]
                for kind in range(2):
                    xs_sc[kind, s, pi * segs_pp:(pi + 1) * segs_pp, :] = (
                        rows_s[:, kind * LANES:(kind + 1) * LANES])
            c0 = pl.multiple_of((j * npg + pi) * PAGE_SIZE, PAGE_SIZE)
            ks_sc[0:LANES, pl.ds(c0, PAGE_SIZE)] = pg[2 * LANES:3 * LANES, :].astype(bf16)
            vs_sc[:, pl.ds(c0, PAGE_SIZE)] = pg[3 * LANES:4 * LANES, :].astype(bf16)

    def compress_rows(xs_sc, step):
        seg0 = pl.multiple_of(step * segs, segs)
        for kind in range(2):
            load = lambda s, kind=kind: xs_sc[kind, s]
            accs = _cmp_hidden_rows(load, wz_ref, kind)
            for g in range(KV_GROUPS):
                c0 = (kind * KV_GROUPS + g) * hw
                h_sc[pl.ds(seg0, segs), c0:c0 + hw] = accs[g]

    @pl.when(j == 0)
    def _():
        unpack_pages(xs_bufs[0])

    for parity in range(2):
        @pl.when((j > 0) & (j % 2 == parity))
        def _(parity=parity):
            unpack_pages(xs_bufs[parity])
            compress_rows(xs_bufs[1 - parity], j - 1)

    @pl.when(j == nj - 1)
    def _():
        compress_rows(xs_bufs[(steps - 1) % 2], j)
        nh = NSA_HEADS
        rows = nh * tdec
        nseg = past // CMP_STRIDE
        nc = (past + tdec) // CMP_STRIDE - CMP_RATIO + 1
        kc, vc = _cmp_finalize(h_sc, bias_ref, w2p_ref)
        qf = q_ref[...]
        q = jnp.concatenate([qf[:, h * LANES:(h + 1) * LANES] for h in range(nh)],
                            axis=0).astype(bf16)
        trow = lax.broadcasted_iota(jnp.int32, (rows, 1), 0) & (tdec - 1)
        qpos = past + trow

        n_idx = lax.broadcasted_iota(jnp.int32, (rows, nseg), 1)
        cmask = (n_idx * CMP_STRIDE + (CMP_LEN - 1) <= qpos) & (n_idx < nc)
        p, inv = _softmax_rows(_dot_nt(q, kc.astype(bf16)), cmask)
        p = p * inv
        o_c = _dot(p.astype(bf16), vc.astype(bf16))

        gr = HPG * tdec
        psums = []
        for g in range(KV_GROUPS):
            ps = p[g * gr:g * gr + tdec]
            for h in range(1, HPG):
                ps = ps + p[g * gr + h * tdec:g * gr + (h + 1) * tdec]
            psums.append(ps)
        imp = _importance(jnp.concatenate(psums, axis=0), m_ref)
        n_past_blk = past // SLC_BLOCK
        blk = lax.broadcasted_iota(jnp.int32, imp.shape, 1)
        forced = (blk == 0) | (blk == n_past_blk - 1)
        score = jnp.concatenate([jnp.where(forced, FORCE_SCORE, imp),
                                 jnp.zeros((LANES - imp.shape[0], imp.shape[1]), f32)], axis=0)
        sel_bias = _select_bias(score, TOP_N - 1)
        q_aug = jnp.concatenate(
            [q, jnp.concatenate([sel_bias[g * tdec:(g + 1) * tdec] for g in range(KV_GROUPS)
                                 for _ in range(HPG)], axis=0).astype(bf16)], axis=1)

        def new_rows(c):
            blk_new = kv_ref[:, c * LANES:(c + 1) * LANES].astype(bf16)
            return jnp.concatenate([blk_new, jnp.zeros((LANES - tdec, LANES), bf16)], axis=0)

        new_mask = lax.broadcasted_iota(jnp.int32, (rows, LANES), 1) <= trow

        def two_part(s_old, v_old_t, c_k, c_v):
            s_new = jnp.where(new_mask, _dot_nt(q, new_rows(c_k)), NEG)
            m = jnp.maximum(jnp.max(s_old, axis=-1, keepdims=True),
                            jnp.max(s_new, axis=-1, keepdims=True))
            p_old = jnp.exp(s_old - m)
            p_new = jnp.exp(s_new - m)
            l = jnp.sum(p_old, axis=-1, keepdims=True) + jnp.sum(p_new, axis=-1, keepdims=True)
            o = _dot_nt(p_old.astype(bf16), v_old_t) + _dot(p_new.astype(bf16), new_rows(c_v))
            return o * (1.0 / l)

        o_s = two_part(_dot(q_aug, ks_sc[...]), vs_sc[...], 2, 3)

        wb = win_ref.shape[1]
        dist = wb + trow - lax.broadcasted_iota(jnp.int32, (rows, wb), 1)
        s_w = _dot(q, win_ref[0:LANES, :].astype(bf16))
        s_w = jnp.where((dist >= 0) & (dist < WINDOW), s_w, NEG)
        o_w = two_part(s_w, win_ref[LANES:2 * LANES, :].astype(bf16), 4, 5)

        gt = gate_ref[...]
        for h in range(nh):
            r0, r1 = h * tdec, (h + 1) * tdec
            c = (h // HPG) * LANES + (h % HPG) * N_GATES
            o = (o_c[r0:r1] * gt[:, c:c + 1] + o_s[r0:r1] * gt[:, c + 1:c + 2]
                 + o_w[r0:r1] * gt[:, c + 2:c + 3])
            o_ref[:, h * LANES:(h + 1) * LANES] = o.astype(o_ref.dtype)


def _attn_sample(page_table, cache, q3, gates3, kv3, win, perm, wz, bias, w2p, imp_m, sel_e):
    bd, n_pages = page_table.shape
    tdec = q3.shape[1]
    past = n_pages * PAGE_SIZE
    npg = PAGES_PER_STEP
    assert n_pages % npg == 0 and past % SLC_BLOCK == 0 and tdec <= CMP_STRIDE
    assert past // SLC_BLOCK == LANES and win.shape[2] <= WINDOW
    nj = n_pages // npg
    nseg = past // CMP_STRIDE
    hw = 2 * KV_GROUPS * CMP_RATIO * CMP_HIDDEN
    page_specs = [
        pl.BlockSpec((None, 4 * KVW, PAGE_SIZE),
                     lambda b, j, pt, p=p: (pt[b * n_pages + j * npg + p], 0, 0))
        for p in range(npg)]
    per_b = lambda a: pl.BlockSpec((None,) + a.shape[1:], lambda b, j, pt: (b,) + (0,) * (a.ndim - 1))
    full = lambda a: pl.BlockSpec(a.shape, lambda b, j, pt: (0,) * a.ndim)
    grid_spec = pltpu.PrefetchScalarGridSpec(
        num_scalar_prefetch=1,
        grid=(bd, nj),
        in_specs=page_specs + [per_b(q3), per_b(gates3), per_b(kv3), per_b(win), full(perm),
                               full(wz), full(bias), full(w2p), full(imp_m), full(sel_e)],
        out_specs=pl.BlockSpec((None, tdec, NSA_HEADS * LANES), lambda b, j, pt: (b, 0, 0)),
        scratch_shapes=[pltpu.VMEM((2, CMP_STRIDE, npg * PAGE_SIZE // CMP_STRIDE, LANES), f32),
                        pltpu.VMEM((2, CMP_STRIDE, npg * PAGE_SIZE // CMP_STRIDE, LANES), f32),
                        pltpu.VMEM((nseg, hw), f32), pltpu.VMEM((2 * LANES, past), bf16),
                        pltpu.VMEM((LANES, past), bf16)],
    )
    return pl.pallas_call(
        functools.partial(_attn_sample_kernel, past=past, tdec=tdec, steps=nj),
        grid_spec=grid_spec,
        out_shape=jax.ShapeDtypeStruct((bd, tdec, NSA_HEADS * LANES), f32),
        compiler_params=pltpu.CompilerParams(dimension_semantics=("arbitrary", "arbitrary"),
                                             vmem_limit_bytes=VMEM_LIMIT),
        name="attn_sample",
    )(page_table.reshape(-1), *([cache] * npg), q3, gates3, kv3, win, perm, wz, bias, w2p, imp_m,
      sel_e)


def _ffn_rows_kernel(x_ref, a_ref, o_ref, wout_ref, g2_ref, gf_ref, wup_ref, cw_ref, cb_ref, wd_ref,
                     y_ref, tail_ref, carry_sc, *, tiles_per_seq):
    i = pl.program_id(0)
    tm = x_ref.shape[0]
    fc = FFN_FC

    @pl.when(i % tiles_per_seq == 0)
    def _():
        carry_sc[...] = jnp.zeros(carry_sc.shape, f32)

    cat = jnp.concatenate([a_ref[...], o_ref[...]], axis=1)
    hp = x_ref[...] + _dot(cat, wout_ref[...])
    h2 = ((hp * _rms_scale(hp)) * g2_ref[...]).astype(bf16)
    row = lax.broadcasted_iota(jnp.int32, (SUBLANES, fc), 0)

    def shifted(a, prev, k):
        rolled = pltpu.roll(a, k, axis=0)
        top = jnp.where(row < k, pltpu.roll(prev, k, axis=0), rolled[:SUBLANES])
        return jnp.concatenate([top, rolled[SUBLANES:]], axis=0)

    acc = None
    acts = []
    for j in range(D_FF // fc):
        halves = []
        for half in range(2):
            cols = slice(half * D_FF + j * fc, half * D_FF + (j + 1) * fc)
            a = _dot(h2, wup_ref[:, cols])
            tail = a[tm - SUBLANES:, :]
            tail_ref[:, cols] = tail
            prev = carry_sc[:, cols]
            s1 = shifted(a, prev, 1)
            s2 = shifted(a, prev, 2)
            carry_sc[:, cols] = tail
            halves.append(s2 * cw_ref[0:1, cols] + s1 * cw_ref[1:2, cols] + a * cw_ref[2:3, cols]
                          + cb_ref[:, cols])
        acts.append((_gelu(halves[0]) * halves[1]).astype(bf16))
        if len(acts) == FFN_DOWN_GROUP or j == D_FF // fc - 1:
            r1 = (j + 1) * fc
            d = _dot(jnp.concatenate(acts, axis=1), wd_ref[r1 - len(acts) * fc:r1, :])
            acc = d if acc is None else acc + d
            acts = []
    out = hp + acc
    y_ref[...] = (out * _rms_scale(out)) * gf_ref[...]


def _ffn_rows(x, a_out, o_pad, wout, g2, gf, wup, cw, cb, wdn, seq_len):
    n = x.shape[0]
    tm = FFN_TM
    assert n % tm == 0 and seq_len % tm == 0 and D_FF % FFN_FC == 0
    ni = n // tm
    row = lambda w: pl.BlockSpec((tm, w), lambda i: (i, 0))
    resident = lambda a: pl.BlockSpec(a.shape, lambda i: (0,) * a.ndim, pipeline_mode=pl.Buffered(1))
    return pl.pallas_call(
        functools.partial(_ffn_rows_kernel, tiles_per_seq=seq_len // tm),
        grid=(ni,),
        in_specs=[row(D_MODEL), row(GM_WIDTH), row(NSA_HEADS * LANES), resident(wout), resident(g2),
                  resident(gf), resident(wup), resident(cw), resident(cb), resident(wdn)],
        out_specs=[row(D_MODEL), pl.BlockSpec((None, SUBLANES, 2 * D_FF), lambda i: (i, 0, 0))],
        out_shape=[jax.ShapeDtypeStruct((n, D_MODEL), f32),
                   jax.ShapeDtypeStruct((ni, SUBLANES, 2 * D_FF), f32)],
        scratch_shapes=[pltpu.VMEM((SUBLANES, 2 * D_FF), f32)],
        compiler_params=pltpu.CompilerParams(dimension_semantics=("arbitrary",),
                                             vmem_limit_bytes=VMEM_LIMIT),
        name="ffn_prompt",
    )(x, a_out, o_pad, wout, g2, gf, wup, cw, cb, wdn)


def _ffn_kernel(*refs, tiles_per_seq, tdec):
    if tdec:
        (x_ref, a_ref, o_ref, wout_ref, g2_ref, gf_ref, wg_ref, wv_ref, cwg_ref, cwv_ref,
         cbg_ref, cbv_ref, wd_ref, pg_ref, pv_ref, y_ref, tg_ref, tv_ref,
         hp_sc, h2_sc, acc_sc) = refs
    else:
        (x_ref, a_ref, o_ref, wout_ref, g2_ref, gf_ref, wg_ref, wv_ref, cwg_ref, cwv_ref,
         cbg_ref, cbv_ref, wd_ref, y_ref, tg_ref, tv_ref,
         hp_sc, h2_sc, acc_sc, cg_sc, cv_sc) = refs
    i = pl.program_id(0)
    j = pl.program_id(1)
    nj = pl.num_programs(1)
    tm = x_ref.shape[0]

    @pl.when(j == 0)
    def _():
        cat = jnp.concatenate([a_ref[...], o_ref[...].astype(bf16)], axis=1)
        hp = x_ref[...] + _dot(cat, wout_ref[...])
        hp_sc[...] = hp
        h2_sc[...] = ((hp * _rms_scale(hp)) * g2_ref[...]).astype(bf16)
        acc_sc[...] = jnp.zeros(acc_sc.shape, f32)

    h2 = h2_sc[...]
    halves = []
    for w_ref, cw_ref, cb_ref, t_ref, extra in (
            (wg_ref, cwg_ref, cbg_ref, tg_ref, pg_ref if tdec else cg_sc),
            (wv_ref, cwv_ref, cbv_ref, tv_ref, pv_ref if tdec else cv_sc)):
        a = _dot(h2, w_ref[...])
        fc = a.shape[1]
        if tdec:
            nb = tm // tdec
            a3 = a.reshape(nb, tdec, fc)
            t_ref[...] = a3[:, tdec - (CONV_W - 1):, :]
            prev = extra[...]
            trow = lax.broadcasted_iota(jnp.int32, (nb, tdec, fc), 1)
            s1 = jnp.where(trow == 0, prev[:, 1:2, :], pltpu.roll(a3, 1, axis=1))
            s2 = jnp.where(trow == 0, prev[:, 0:1, :],
                           jnp.where(trow == 1, prev[:, 1:2, :], pltpu.roll(a3, 2, axis=1)))
            c = (s2 * cw_ref[0:1, :] + s1 * cw_ref[1:2, :] + a3 * cw_ref[2:3, :]
                 + cb_ref[...]).reshape(tm, fc)
        else:
            tail = a[tm - SUBLANES:, :]
            t_ref[...] = tail

            @pl.when(i % tiles_per_seq == 0)
            def _(extra=extra):
                extra[j] = jnp.zeros(extra.shape[1:], f32)

            prev = extra[j]
            row = lax.broadcasted_iota(jnp.int32, (tm, fc), 0)
            s1 = jnp.where(row == 0, prev[SUBLANES - 1:SUBLANES, :], pltpu.roll(a, 1, axis=0))
            s2 = jnp.where(row == 0, prev[SUBLANES - 2:SUBLANES - 1, :],
                           jnp.where(row == 1, prev[SUBLANES - 1:SUBLANES, :],
                                     pltpu.roll(a, 2, axis=0)))
            extra[j] = tail
            c = s2 * cw_ref[0:1, :] + s1 * cw_ref[1:2, :] + a * cw_ref[2:3, :] + cb_ref[...]
        halves.append(c)
    act = (_gelu(halves[0]) * halves[1]).astype(bf16)
    acc_sc[...] += _dot(act, wd_ref[...])

    @pl.when(j == nj - 1)
    def _():
        out = hp_sc[...] + acc_sc[...]
        y_ref[...] = (out * _rms_scale(out)) * gf_ref[...]


def _ffn(x, a_out, o_pad, wout, g2, gf, wup, cw, cb, wdn, state, seq_len, tdec, tm):
    n = x.shape[0]
    fc = FFN_FC
    assert n % tm == 0 and D_FF % fc == 0
    ni, nj = n // tm, D_FF // fc
    row = lambda w: pl.BlockSpec((tm, w), lambda i, j: (i, 0))
    full = lambda a: pl.BlockSpec(a.shape, lambda i, j: (0,) * a.ndim)
    gate_col = lambda r: pl.BlockSpec((r, fc), lambda i, j: (0, j))
    val_col = lambda r: pl.BlockSpec((r, fc), lambda i, j: (0, nj + j))
    in_specs = [row(D_MODEL), row(GM_WIDTH), row(NSA_HEADS * LANES), full(wout), full(g2), full(gf),
                gate_col(D_MODEL), val_col(D_MODEL), gate_col(CONV_W), val_col(CONV_W),
                gate_col(1), val_col(1), pl.BlockSpec((fc, D_MODEL), lambda i, j: (j, 0))]
    args = [x, a_out, o_pad, wout, g2, gf, wup, wup, cw, cw, cb, cb, wdn]
    scratch = [pltpu.VMEM((tm, D_MODEL), f32), pltpu.VMEM((tm, D_MODEL), bf16),
               pltpu.VMEM((tm, D_MODEL), f32)]
    if tdec:
        assert tm % tdec == 0 and tdec == SUBLANES
        nb = tm // tdec
        in_specs += [pl.BlockSpec((nb, CONV_W - 1, fc), lambda i, j: (i, 0, j)),
                     pl.BlockSpec((nb, CONV_W - 1, fc), lambda i, j: (i, 0, nj + j))]
        args += [state, state]
        tail_shape = jax.ShapeDtypeStruct((n // tdec, CONV_W - 1, D_FF), f32)
        tail_spec = pl.BlockSpec((nb, CONV_W - 1, fc), lambda i, j: (i, 0, j))
        tiles_per_seq = 0
    else:
        assert seq_len % tm == 0
        tiles_per_seq = seq_len // tm
        tail_shape = jax.ShapeDtypeStruct((ni, SUBLANES, D_FF), f32)
        tail_spec = pl.BlockSpec((None, SUBLANES, fc), lambda i, j: (i, 0, j))
        scratch += [pltpu.VMEM((nj, SUBLANES, fc), f32), pltpu.VMEM((nj, SUBLANES, fc), f32)]
    return pl.pallas_call(
        functools.partial(_ffn_kernel, tiles_per_seq=tiles_per_seq, tdec=tdec),
        grid=(ni, nj),
        in_specs=in_specs,
        out_specs=[row(D_MODEL), tail_spec, tail_spec],
        out_shape=[jax.ShapeDtypeStruct((n, D_MODEL), f32), tail_shape, tail_shape],
        scratch_shapes=scratch,
        compiler_params=pltpu.CompilerParams(dimension_semantics=("arbitrary", "arbitrary"),
                                             vmem_limit_bytes=VMEM_LIMIT),
        name="ffn_sample" if tdec else "ffn_prompt",
    )(*args)


def _pad_group_lanes(w, heads_axis_len):
    lead = w.shape[:-1]
    w = w.reshape(lead + (heads_axis_len, HEAD_DIM))
    z = jnp.zeros_like(w)
    first = (jnp.arange(heads_axis_len) < HPG)[:, None]
    lo = jnp.where(first, w, z)
    hi = jnp.where(first, z, w)
    return jnp.concatenate([lo, hi], axis=-1).reshape(lead + (heads_axis_len * LANES,))


def _importance_matrix(nseg, n_blk):
    m = np.zeros((nseg, n_blk), np.float32)
    for n in range(nseg - CMP_RATIO + 1):
        for r in range(CMP_RATIO):
            b = (n + r) // SEG_PER_SLC
            if b < n_blk:
                m[n, b] += 1.0
    return jnp.asarray(m, bf16)


def _block_one_hot(n_blk, n_keys):
    e = (np.arange(n_keys)[None, :] // SLC_BLOCK) == np.arange(n_blk)[:, None]
    return e.astype(np.float32)


def _segment_permutation():
    segs = PAGE_SIZE // CMP_STRIDE
    p = np.zeros((PAGE_SIZE, PAGE_SIZE), np.float32)
    for s in range(CMP_STRIDE):
        for n in range(segs):
            p[s * segs + n, n * CMP_STRIDE + s] = 1.0
    return jnp.asarray(p, bf16)


def kernel(x_prompt, x_sample, cache_kv, cache_win, state_conv, page_table, norm1_g, w_in, gm_ln_g,
           gm_ln_b, gm_ws, gm_bs, cmp_pe, cmp_w1, cmp_w2, w_out, norm2_g, w_up, conv_w, conv_b,
           w_down, final_g):
    depth = w_in.shape[0]
    assert depth == 1
    l = 0
    B, T, _ = x_prompt.shape
    Bd, Td, _ = x_sample.shape
    assert T % CHUNK == 0 and Td == SUBLANES and CHUNK % Td == 0

    wi = w_in[l]
    o_q = 2 * GM_WIDTH
    o_kv = o_q + NSA_WIDTH
    o_gl = o_kv + N_KV_BRANCH * KVW
    wuv = wi[:, :o_q].astype(bf16)
    wq = _pad_group_lanes(wi[:, o_q:o_kv] * (HEAD_DIM ** -0.5), NSA_HEADS).astype(bf16)
    wkv = wi[:, o_kv:o_gl].astype(bf16)
    gpg = HPG * N_GATES
    wgl = jnp.concatenate(
        [jnp.pad(wi[:, o_gl + g * gpg:o_gl + (g + 1) * gpg], ((0, 0), (0, LANES - gpg)))
         for g in range(KV_GROUPS)], axis=1).astype(bf16)
    g1 = norm1_g[l].reshape(1, D_MODEL)
    lng = gm_ln_g[l].reshape(1, GM_WIDTH)
    lnb = gm_ln_b[l].reshape(1, GM_WIDTH)
    causal = jnp.tril(jnp.ones((CHUNK, CHUNK), bool))
    wmix_p = jnp.where(causal, gm_ws[l], 0.0).astype(bf16)
    bmix_p = jnp.broadcast_to(gm_bs[l][:, :, None], (GM_GROUPS, CHUNK, GM_GD))
    reps = CHUNK // Td
    ws_d = jnp.where(causal[:Td, :Td], gm_ws[l][:, :Td, :Td], 0.0)
    eye = jnp.eye(reps, dtype=f32)
    wmix_s = (eye[None, :, None, :, None] * ws_d[:, None, :, None, :]).reshape(
        GM_GROUPS, CHUNK, CHUNK).astype(bf16)
    bmix_s = jnp.broadcast_to(jnp.tile(gm_bs[l][:, :Td], (1, reps))[:, :, None],
                              (GM_GROUPS, CHUNK, GM_GD))

    w1r = cmp_w1[l].reshape(2, CMP_RATIO, CMP_STRIDE, HEAD_DIM, CMP_HIDDEN)
    w1cat = jnp.concatenate([w1r[:, r] for r in range(CMP_RATIO)], axis=-1)
    wz = w1cat.reshape(2, CMP_STRIDE // 4, 4 * HEAD_DIM, CMP_RATIO * CMP_HIDDEN).astype(bf16)
    pe8 = jnp.broadcast_to(cmp_pe[l].reshape(2, 1, CMP_LEN * HEAD_DIM), (2, SUBLANES, CMP_LEN * HEAD_DIM))
    cbias = _cmp_bias(pe8, cmp_w1[l].reshape(2, CMP_LEN * HEAD_DIM, CMP_HIDDEN))
    w2 = cmp_w2[l]
    z2 = jnp.zeros_like(w2)
    w2p = jnp.stack([jnp.concatenate([w2, z2], axis=-1),
                     jnp.concatenate([z2, w2], axis=-1)], axis=1).astype(bf16)

    wo = w_out[l]
    wout = jnp.concatenate([wo[:GM_WIDTH], _pad_group_lanes(wo[GM_WIDTH:].T, NSA_HEADS).T],
                           axis=0).astype(bf16)
    g2 = norm2_g[l].reshape(1, D_MODEL)
    gf = final_g.reshape(1, D_MODEL)
    wup = w_up[l].astype(bf16)
    wdn = w_down[l].astype(bf16)
    cw = conv_w[l]
    cb = conv_b[l].reshape(1, 2 * D_FF)

    xp = x_prompt.reshape(B * T, D_MODEL)
    a_p, _, q_p, kv_p, kvb_p, gate_p = _project(xp, g1, wuv, wq, wkv, wgl, lng, lnb, wmix_p, bmix_p, bf16)
    hid_p = _cmp_hidden(kv_p, wz)
    kc_p, vc_p = _cmp_final(hid_p, cbias, w2p, B)
    imp_m = _importance_matrix(T // CMP_STRIDE, T // SLC_BLOCK)
    one_hot_t = jnp.asarray(_block_one_hot(T // SLC_BLOCK, T).T, bf16)
    o_p = _attn_prompt(q_p, gate_p, kc_p, vc_p, kvb_p, imp_m, one_hot_t, B)
    y_p, tail_p = _ffn_rows(xp, a_p, o_p, wout, g2, gf, wup, cw, cb, wdn, T)
    tiles = T // FFN_TM
    conv_prompt = tail_p.reshape(B, tiles, SUBLANES, 2 * D_FF)[:, tiles - 1, SUBLANES - (CONV_W - 1):]
    kv6 = kv_p.reshape(B, T, N_KV_BRANCH, KV_GROUPS, HEAD_DIM)
    wp = min(WINDOW, T)

    n_pages = page_table.shape[1]
    past = n_pages * PAGE_SIZE
    xs = x_sample.reshape(Bd * Td, D_MODEL)
    a_s, vn_s, q_s, kv_s, _, gate_s = _project(xs, g1, wuv, wq, wkv, wgl, lng, lnb, wmix_s, bmix_s, f32)
    cache = jnp.transpose(cache_kv[l], (0, 2, 3, 4, 1)).reshape(cache_kv.shape[1], 4 * KVW, PAGE_SIZE)
    wb = cache_win.shape[2]
    win = jnp.transpose(cache_win[l], (0, 2, 3, 4, 1)).reshape(Bd, 2 * KVW, wb)
    o_s = _attn_sample(page_table, cache, q_s.reshape(Bd, Td, -1), gate_s.reshape(Bd, Td, -1),
                       kv_s.reshape(Bd, Td, -1), win, _segment_permutation(), wz, cbias, w2p,
                       _importance_matrix(past // CMP_STRIDE, past // SLC_BLOCK),
                       jnp.asarray(_block_one_hot(past // SLC_BLOCK, past), bf16))
    tm_s = min(FFN_TM, Bd * Td)
    y_s, tg_s, tv_s = _ffn(xs, a_s, o_s.reshape(Bd * Td, -1), wout, g2, gf, wup, cw, cb, wdn,
                           state_conv[l], Td, Td, tm_s)
    kv6_s = kv_s.reshape(Bd, Td, N_KV_BRANCH, KV_GROUPS, HEAD_DIM)
    win_new = jnp.concatenate([cache_win[l], kv6_s[:, :, 4:]], axis=1)[:, Td:]

    return (y_p.reshape(B, T, D_MODEL),
            y_s.reshape(Bd, Td, D_MODEL),
            kv6[None, :, :, :4],
            kv6_s[None, :, :, :4],
            kv6[None, :, T - wp:, 4:],
            win_new[None],
            conv_prompt[None],
            jnp.concatenate([tg_s, tv_s], axis=-1)[None],
            vn_s.reshape(1, Bd, Td, GM_WIDTH))
```

```python
import functools
import math

import numpy as np
import jax
import jax.numpy as jnp
from jax import lax
from jax.experimental import pallas as pl
from jax.experimental.pallas import tpu as pltpu

D_MODEL = 1024
GM_WIDTH = D_MODEL // 2
GM_GROUPS = 4
GM_GD = GM_WIDTH // GM_GROUPS
CHUNK = 128
HEAD_DIM = 64
NSA_WIDTH = D_MODEL - GM_WIDTH
NSA_HEADS = NSA_WIDTH // HEAD_DIM
KV_GROUPS = 2
HPG = NSA_HEADS // KV_GROUPS
KVW = KV_GROUPS * HEAD_DIM
N_KV_BRANCH = 6
N_GATES = 3
CMP_LEN = 32
CMP_STRIDE = 16
CMP_RATIO = CMP_LEN // CMP_STRIDE
CMP_HIDDEN = 256
SLC_BLOCK = 64
SLC_SHIFT = SLC_BLOCK.bit_length() - 1
SEG_PER_SLC = SLC_BLOCK // CMP_STRIDE
TOP_N = 16
WINDOW = 512
Q_BLOCK = 128
FORCE_SCORE = 1e9
D_FF = ((8 * D_MODEL // 3 + 127) // 128) * 128
CONV_W = 3
EPS = 1e-6
PAGE_SIZE = 128

LANES = 128
SUBLANES = 8
NEG = -1e30
VMEM_LIMIT = 56 * 1024 * 1024

PROJ_TM = 256
CMP_SEGS = 128
SEL_KC = 512
FFN_TM = 512
FFN_FC = 256
FFN_DOWN_GROUP = 4
PAGES_PER_STEP = 16
PAGE_SLOTS = 4

f32 = jnp.float32
bf16 = jnp.bfloat16

_NT = (((1,), (1,)), ((), ()))


def _gelu(x):
    c = math.sqrt(2.0 / math.pi)
    return x * (0.5 * (1.0 + jnp.tanh(c * (x + 0.044715 * (x * x * x)))))


def _dot(a, b):
    return jnp.dot(a, b, preferred_element_type=f32)


def _dot_nt(a, b):
    return lax.dot_general(a, b, _NT, preferred_element_type=f32)


def _rms_scale(x):
    return lax.rsqrt(jnp.mean(x * x, axis=-1, keepdims=True) + EPS)


def _project_kernel(x_ref, g1_ref, wuv_ref, wq_ref, wkv_ref, wgl_ref, lng_ref, lnb_ref,
                    wmix_ref, bmix_ref, a_ref, vn_ref, q_ref, kv_ref, kvb_ref, gate_ref):
    x = x_ref[...]
    h = ((x * _rms_scale(x)) * g1_ref[...]).astype(bf16)
    uv = _dot(h, wuv_ref[...])
    tm = x.shape[0]
    for g in range(GM_GROUPS):
        lo, hi = g * GM_GD, (g + 1) * GM_GD
        u = _gelu(uv[:, lo:hi])
        v = _gelu(uv[:, GM_WIDTH + lo:GM_WIDTH + hi])
        mu = jnp.mean(v, axis=-1, keepdims=True)
        d = v - mu
        var = jnp.mean(d * d, axis=-1, keepdims=True)
        vn = (d * lax.rsqrt(var + EPS)) * lng_ref[:, lo:hi] + lnb_ref[:, lo:hi]
        vn_ref[:, lo:hi] = vn
        vnb = vn.astype(bf16)
        for c in range(tm // CHUNK):
            r0, r1 = c * CHUNK, (c + 1) * CHUNK
            mixed = _dot(wmix_ref[g], vnb[r0:r1]) + bmix_ref[g]
            a_ref[r0:r1, lo:hi] = (u[r0:r1] * mixed).astype(a_ref.dtype)
    q_ref[...] = _dot(h, wq_ref[...]).astype(q_ref.dtype)
    kv = _dot(h, wkv_ref[...])
    kv_ref[...] = kv
    kvb_ref[...] = kv.astype(bf16)
    gl = _dot(h, wgl_ref[...])
    gate_ref[...] = 1.0 / (1.0 + jnp.exp(-gl))


def _project(x, g1, wuv, wq, wkv, wgl, lng, lnb, wmix, bmix, q_dtype):
    n = x.shape[0]
    tm = PROJ_TM
    assert n % tm == 0
    row = lambda w: pl.BlockSpec((tm, w), lambda i: (i, 0))
    full = lambda a: pl.BlockSpec(a.shape, lambda i: (0,) * a.ndim)
    kvw = N_KV_BRANCH * KVW
    return pl.pallas_call(
        _project_kernel,
        grid=(n // tm,),
        in_specs=[row(D_MODEL), full(g1), full(wuv), full(wq), full(wkv), full(wgl),
                  full(lng), full(lnb), full(wmix), full(bmix)],
        out_specs=[row(GM_WIDTH), row(GM_WIDTH), row(NSA_HEADS * LANES), row(kvw), row(kvw),
                   row(KV_GROUPS * LANES)],
        out_shape=[jax.ShapeDtypeStruct((n, GM_WIDTH), bf16),
                   jax.ShapeDtypeStruct((n, GM_WIDTH), f32),
                   jax.ShapeDtypeStruct((n, NSA_HEADS * LANES), q_dtype),
                   jax.ShapeDtypeStruct((n, kvw), f32),
                   jax.ShapeDtypeStruct((n, kvw), bf16),
                   jax.ShapeDtypeStruct((n, KV_GROUPS * LANES), f32)],
        compiler_params=pltpu.CompilerParams(dimension_semantics=("arbitrary",),
                                             vmem_limit_bytes=VMEM_LIMIT),
        name="project",
    )(x, g1, wuv, wq, wkv, wgl, lng, lnb, wmix, bmix)


def _cmp_hidden_rows(load_rows, w4_ref, kind):
    accs = [None] * KV_GROUPS
    first_half = None
    for s4 in range(CMP_STRIDE // 4):
        pairs = []
        for s in (4 * s4, 4 * s4 + 2):
            a, b = load_rows(s), load_rows(s + 1)
            if first_half is None:
                first_half = lax.broadcasted_iota(jnp.int32, a.shape, 1) < HEAD_DIM
            pairs.append((jnp.where(first_half, a, pltpu.roll(b, HEAD_DIM, axis=1)),
                          jnp.where(first_half, pltpu.roll(a, HEAD_DIM, axis=1), b)))
        for g in range(KV_GROUPS):
            lhs = jnp.concatenate([pairs[0][g], pairs[1][g]], axis=1).astype(bf16)
            d = _dot(lhs, w4_ref[kind, s4])
            accs[g] = d if accs[g] is None else accs[g] + d
    return accs


def _cmp_hidden_kernel(xk_ref, xv_ref, wz_ref, h_ref):
    segs = h_ref.shape[0]
    hw = CMP_RATIO * CMP_HIDDEN
    for kind, x_ref in enumerate((xk_ref, xv_ref)):
        load = lambda s, x_ref=x_ref: x_ref[pl.ds(s, segs, stride=CMP_STRIDE), :]
        accs = _cmp_hidden_rows(load, wz_ref, kind)
        for g in range(KV_GROUPS):
            c0 = (kind * KV_GROUPS + g) * hw
            h_ref[:, c0:c0 + hw] = accs[g]


def _cmp_hidden(kv, wz):
    n = kv.shape[0]
    rows = CMP_SEGS * CMP_STRIDE
    assert n % rows == 0
    hw = 2 * KV_GROUPS * CMP_RATIO * CMP_HIDDEN
    return pl.pallas_call(
        _cmp_hidden_kernel,
        grid=(n // rows,),
        in_specs=[pl.BlockSpec((rows, LANES), lambda i: (i, 0)),
                  pl.BlockSpec((rows, LANES), lambda i: (i, 1)),
                  pl.BlockSpec(wz.shape, lambda i: (0,) * wz.ndim)],
        out_specs=pl.BlockSpec((CMP_SEGS, hw), lambda i: (i, 0)),
        out_shape=jax.ShapeDtypeStruct((n // CMP_STRIDE, hw), f32),
        compiler_params=pltpu.CompilerParams(dimension_semantics=("arbitrary",),
                                             vmem_limit_bytes=VMEM_LIMIT),
        name="cmp_hidden",
    )(kv, kv, wz)


def _cmp_bias_kernel(pe_ref, w1_ref, o_ref):
    for kind in range(2):
        o_ref[kind] = _dot(pe_ref[kind].astype(bf16), w1_ref[kind].astype(bf16))


def _cmp_bias(pe8, w1flat):
    return pl.pallas_call(
        _cmp_bias_kernel,
        out_shape=jax.ShapeDtypeStruct((2, SUBLANES, CMP_HIDDEN), f32),
        name="cmp_bias",
    )(pe8, w1flat)


def _cmp_finalize(hfull, bias_ref, w2p_ref):
    nseg = hfull.shape[0]
    hw = CMP_RATIO * CMP_HIDDEN
    outs = []
    for kind in range(2):
        acc = None
        for g in range(KV_GROUPS):
            c0 = (kind * KV_GROUPS + g) * hw
            h0 = hfull[:, c0:c0 + CMP_HIDDEN]
            h1 = hfull[:, c0 + CMP_HIDDEN:c0 + hw]
            hid = h0 + pltpu.roll(h1, nseg - 1, axis=0) + bias_ref[kind, 0:1, :]
            d = _dot(_gelu(hid).astype(bf16), w2p_ref[kind, g])
            acc = d if acc is None else acc + d
        outs.append(acc)
    return outs


def _cmp_final_kernel(h_ref, bias_ref, w2p_ref, kc_ref, vc_ref):
    kc, vc = _cmp_finalize(h_ref, bias_ref, w2p_ref)
    kc_ref[...] = kc.astype(bf16)
    vc_ref[...] = vc.astype(bf16)


def _cmp_final(hid, bias, w2p, batch):
    nseg = hid.shape[0] // batch
    hw = hid.shape[1]
    full = lambda a: pl.BlockSpec(a.shape, lambda b: (0,) * a.ndim)
    out = jax.ShapeDtypeStruct((batch, nseg, LANES), bf16)
    return pl.pallas_call(
        _cmp_final_kernel,
        grid=(batch,),
        in_specs=[pl.BlockSpec((nseg, hw), lambda b: (b, 0)), full(bias), full(w2p)],
        out_specs=[pl.BlockSpec((None, nseg, LANES), lambda b: (b, 0, 0))] * 2,
        out_shape=[out, out],
        compiler_params=pltpu.CompilerParams(dimension_semantics=("arbitrary",),
                                             vmem_limit_bytes=VMEM_LIMIT),
        name="cmp_final",
    )(hid, bias, w2p)


def _softmax_rows(s, mask):
    s = jnp.where(mask, s, NEG)
    m = jnp.max(s, axis=-1, keepdims=True)
    p = jnp.where(mask, jnp.exp(s - m), 0.0)
    l = jnp.sum(p, axis=-1, keepdims=True)
    return p, 1.0 / jnp.maximum(l, 1e-30)


def _importance(psum, m_ref):
    hi = psum.astype(bf16)
    lo = (psum - hi.astype(f32)).astype(bf16)
    return _dot(hi, m_ref[...]) + _dot(lo, m_ref[...])


def _select_bias(score, rounds):
    st = score.T
    blk = lax.broadcasted_iota(jnp.int32, st.shape, 0).astype(f32)

    def body(_, carry):
        s, bias = carry
        m = jnp.max(s, axis=0, keepdims=True)
        idx = jnp.min(jnp.where(s == m, blk, float(st.shape[0])), axis=0, keepdims=True)
        pick = blk == idx
        return jnp.where(pick, -jnp.inf, s), jnp.where(pick, 0.0, bias)

    _, bias = lax.fori_loop(0, rounds, body, (st, jnp.full(st.shape, NEG, f32)), unroll=True)
    return bias.T


def _with_ones(v):
    return jnp.concatenate([v, jnp.ones(v.shape, v.dtype)], axis=1)


def _attn_prompt_kernel(q_ref, gate_ref, kc_ref, vc_ref, ks_ref, vs_ref, kw_ref, vw_ref,
                        m_ref, e_ref, o_ref, acc_sc, m_sc, s0_sc, s1_sc, p0_sc, p1_sc, a0_sc, a1_sc,
                        *, nc):
    i = pl.program_id(2)
    qb = Q_BLOCK
    q0 = i * qb
    head = lambda h: slice(h * qb, (h + 1) * qb)
    q = jnp.concatenate([q_ref[:, h * LANES:(h + 1) * LANES] for h in range(HPG)], axis=0)
    qpos = q0 + lax.broadcasted_iota(jnp.int32, (qb, 1), 0)

    kc = kc_ref[...]
    nseg = kc.shape[0]
    n_idx = lax.broadcasted_iota(jnp.int32, (qb, nseg), 1)
    cmask = (n_idx * CMP_STRIDE + (CMP_LEN - 1) <= qpos) & (n_idx < nc)
    s_c = _dot_nt(q, kc)
    p_c = []
    psum = None
    for h in range(HPG):
        p, inv = _softmax_rows(s_c[head(h)], cmask)
        p = p * inv
        p_c.append(p.astype(bf16))
        psum = p if psum is None else psum + p
    o_c = _dot(jnp.concatenate(p_c, axis=0), vc_ref[...])

    wlen = WINDOW + qb
    w0 = pl.multiple_of(jnp.maximum(q0 - WINDOW, 0), qb)
    dist = qpos - (w0 + lax.broadcasted_iota(jnp.int32, (qb, wlen), 1))
    wmask = (dist >= 0) & (dist < WINDOW)
    s_w = _dot_nt(q, kw_ref[pl.ds(w0, wlen), :])
    p_w = []
    for h in range(HPG):
        s = jnp.where(wmask, s_w[head(h)], NEG)
        p_w.append(jnp.exp(s - jnp.max(s, axis=-1, keepdims=True)).astype(bf16))
    o_w = _dot(jnp.concatenate(p_w, axis=0), _with_ones(vw_ref[pl.ds(w0, wlen), :]))

    imp = _importance(psum, m_ref)
    n_slc = imp.shape[1]
    blk = lax.broadcasted_iota(jnp.int32, (qb, n_slc), 1)
    cur = jnp.right_shift(q0 + lax.broadcasted_iota(jnp.int32, (qb, n_slc), 0), SLC_SHIFT)
    forced = (blk == 0) | (blk == cur) | (blk == cur - 1)
    score = jnp.where(forced, FORCE_SCORE, jnp.where(blk <= cur, imp, -jnp.inf))
    sel_bias = _select_bias(score, min(TOP_N, n_slc)).astype(bf16)

    q_aug = jnp.concatenate([q, jnp.concatenate([sel_bias] * HPG, axis=0)], axis=1)
    acc_sc[...] = jnp.zeros(acc_sc.shape, f32)
    m_sc[...] = jnp.full(m_sc.shape, NEG, f32)
    kc_len = SEL_KC

    s_bufs = (s0_sc, s1_sc)
    p_bufs = (p0_sc, p1_sc)
    a_bufs = (a0_sc, a1_sc)

    def scores(c, par):
        k0 = pl.multiple_of(c * kc_len, kc_len)
        k_aug = jnp.concatenate([ks_ref[pl.ds(k0, kc_len), :], e_ref[pl.ds(k0, kc_len), :]], axis=1)
        s_bufs[par][...] = _dot_nt(q_aug, k_aug)

    def numerators(c, par, causal):
        if causal:
            k0 = c * kc_len
            ok = k0 + lax.broadcasted_iota(jnp.int32, (qb, kc_len), 1) <= qpos
        for h in range(HPG):
            s = s_bufs[par][head(h), :]
            if causal:
                s = jnp.where(ok, s, NEG)
            m_old = m_sc[head(h), :]
            m_new = jnp.maximum(m_old, jnp.max(s, axis=-1, keepdims=True))
            p_bufs[par][head(h), :] = jnp.exp(s - m_new).astype(bf16)
            a_bufs[par][head(h), :] = jnp.exp(m_old - m_new)
            m_sc[head(h), :] = m_new

    def value_sums(c, par):
        k0 = pl.multiple_of(c * kc_len, kc_len)
        pv = _dot(p_bufs[par][...], _with_ones(vs_ref[pl.ds(k0, kc_len), :]))
        acc_sc[...] = a_bufs[par][...] * acc_sc[...] + pv

    def stage(c, par):
        scores(c + 1, 1 - par)
        value_sums(jnp.maximum(c - 1, 0), 1 - par)
        numerators(c, par, False)

    def last_stage(c, par):
        value_sums(jnp.maximum(c - 1, 0), 1 - par)
        numerators(c, par, True)
        value_sums(c, par)

    p_bufs[1][...] = jnp.zeros(p_bufs[1].shape, bf16)
    a_bufs[1][...] = jnp.ones(a_bufs[1].shape, f32)

    def body(k, carry):
        stage(2 * k, 0)
        stage(2 * k + 1, 1)
        return carry

    c_diag = q0 // kc_len
    scores(0, 0)
    lax.fori_loop(0, c_diag // 2, body, 0)

    @pl.when(c_diag % 2 == 0)
    def _():
        last_stage(c_diag, 0)

    @pl.when(c_diag % 2 == 1)
    def _():
        stage(c_diag - 1, 0)
        last_stage(c_diag, 1)

    gt = gate_ref[...]
    for h in range(HPG):
        ow = o_w[head(h)]
        acc = acc_sc[head(h), :]
        c = h * N_GATES
        o = (o_c[head(h)] * gt[:, c:c + 1]
             + acc[:, :LANES] * (1.0 / acc[:, LANES:]) * gt[:, c + 1:c + 2]
             + ow[:, :LANES] * (1.0 / ow[:, LANES:]) * gt[:, c + 2:c + 3])
        o_ref[:, h * LANES:(h + 1) * LANES] = o.astype(o_ref.dtype)


def _attn_prompt(qpad, gates, kc, vc, kvb, imp_m, sel_e, batch):
    n = qpad.shape[0]
    t = n // batch
    qb = Q_BLOCK
    nqb = t // qb
    assert t % SEL_KC == 0 and t >= WINDOW + qb
    nseg = t // CMP_STRIDE
    nc = nseg - CMP_RATIO + 1
    gw = HPG * LANES
    col = lambda c: pl.BlockSpec((t, LANES), lambda b, g, i, c=c: (b, c))
    cmp_spec = pl.BlockSpec((None, nseg, LANES), lambda b, g, i: (b, 0, 0))
    full = lambda a: pl.BlockSpec(a.shape, lambda b, g, i: (0,) * a.ndim)
    return pl.pallas_call(
        functools.partial(_attn_prompt_kernel, nc=nc),
        grid=(batch, KV_GROUPS, nqb),
        in_specs=[pl.BlockSpec((qb, gw), lambda b, g, i: (b * nqb + i, g)),
                  pl.BlockSpec((qb, LANES), lambda b, g, i: (b * nqb + i, g)),
                  cmp_spec, cmp_spec, col(2), col(3), col(4), col(5),
                  full(imp_m), full(sel_e)],
        out_specs=pl.BlockSpec((qb, gw), lambda b, g, i: (b * nqb + i, g)),
        out_shape=jax.ShapeDtypeStruct((n, NSA_HEADS * LANES), bf16),
        scratch_shapes=[pltpu.VMEM((HPG * qb, 2 * LANES), f32), pltpu.VMEM((HPG * qb, 1), f32),
                        pltpu.VMEM((HPG * qb, SEL_KC), f32), pltpu.VMEM((HPG * qb, SEL_KC), f32),
                        pltpu.VMEM((HPG * qb, SEL_KC), bf16), pltpu.VMEM((HPG * qb, SEL_KC), bf16),
                        pltpu.VMEM((HPG * qb, 1), f32), pltpu.VMEM((HPG * qb, 1), f32)],
        compiler_params=pltpu.CompilerParams(
            dimension_semantics=("arbitrary", "arbitrary", "arbitrary"),
            vmem_limit_bytes=VMEM_LIMIT),
        name="attn_prompt",
    )(qpad, gates, kc, vc, kvb, kvb, kvb, kvb, imp_m, sel_e)


def _attn_sample_kernel(pt_ref, cache_ref, q_ref, gate_ref, kv_ref, win_ref, perm_ref, wz_ref,
                        bias_ref, w2p_ref, m_ref, e_ref, o_ref, pages_sc, page_sem, xs0_sc, xs1_sc,
                        h_sc, ks_sc, vs_sc, *, past, tdec, steps):
    npg = PAGES_PER_STEP
    j = pl.program_id(1)
    nj = pl.num_programs(1)
    segs_pp = PAGE_SIZE // CMP_STRIDE
    segs = npg * segs_pp
    hw = CMP_RATIO * CMP_HIDDEN
    xs_bufs = (xs0_sc, xs1_sc)

    gstep = pl.program_id(0) * steps + j
    total_steps = pl.num_programs(0) * steps

    def page_copy(step, pi):
        slot = step % PAGE_SLOTS
        return pltpu.make_async_copy(cache_ref.at[pt_ref[step * npg + pi]],
                                     pages_sc.at[slot * npg + pi], page_sem.at[slot])

    def fetch(step):
        for pi in range(npg):
            page_copy(step, pi).start()

    @pl.when(gstep == 0)
    def _():
        for s in range(PAGE_SLOTS - 1):
            fetch(s)

    @pl.when(gstep + (PAGE_SLOTS - 1) < total_steps)
    def _():
        fetch(gstep + (PAGE_SLOTS - 1))

    for pi in range(npg):
        page_copy(gstep, pi).wait()
    slot0 = (gstep % PAGE_SLOTS) * npg
    pages = [pages_sc.at[slot0 + pi] for pi in range(npg)]

    @pl.when((pl.program_id(0) == 0) & (j == 0))
    def _():
        ks_sc[LANES:2 * LANES, :] = e_ref[...]

    def unpack_pages(xs_sc):
        perm = perm_ref[...]
        for pi, pg in enumerate(pages):
            xp = _dot_nt(perm, pg[0:2 * LANES, :].astype(bf16))
            for s in range(CMP_STRIDE):
                rows_s = xp[s * segs_pp:(s + 1) * segs_pp]
                for kind in range(2):
                    xs_sc[kind, s, pi * segs_pp:(pi + 1) * segs_pp, :] = (
                        rows_s[:, kind * LANES:(kind + 1) * LANES])
            c0 = pl.multiple_of((j * npg + pi) * PAGE_SIZE, PAGE_SIZE)
            ks_sc[0:LANES, pl.ds(c0, PAGE_SIZE)] = pg[2 * LANES:3 * LANES, :].astype(bf16)
            vs_sc[:, pl.ds(c0, PAGE_SIZE)] = pg[3 * LANES:4 * LANES, :].astype(bf16)

    def compress_rows(xs_sc, step):
        seg0 = pl.multiple_of(step * segs, segs)
        for kind in range(2):
            load = lambda s, kind=kind: xs_sc[kind, s]
            accs = _cmp_hidden_rows(load, wz_ref, kind)
            for g in range(KV_GROUPS):
                c0 = (kind * KV_GROUPS + g) * hw
                h_sc[pl.ds(seg0, segs), c0:c0 + hw] = accs[g]

    nh = NSA_HEADS
    rows = nh * tdec

    def queries():
        qf = q_ref[...]
        q = jnp.concatenate([qf[:, h * LANES:(h + 1) * LANES] for h in range(nh)],
                            axis=0).astype(bf16)
        return q, lax.broadcasted_iota(jnp.int32, (rows, 1), 0) & (tdec - 1)

    def new_rows(c):
        blk_new = kv_ref[:, c * LANES:(c + 1) * LANES].astype(bf16)
        return jnp.concatenate([blk_new, jnp.zeros((LANES - tdec, LANES), bf16)], axis=0)

    def two_part(q, trow, s_old, v_old_t, c_k, c_v):
        new_mask = lax.broadcasted_iota(jnp.int32, (rows, LANES), 1) <= trow
        s_new = jnp.where(new_mask, _dot_nt(q, new_rows(c_k)), NEG)
        m = jnp.maximum(jnp.max(s_old, axis=-1, keepdims=True),
                        jnp.max(s_new, axis=-1, keepdims=True))
        p_old = jnp.exp(s_old - m)
        p_new = jnp.exp(s_new - m)
        l = jnp.sum(p_old, axis=-1, keepdims=True) + jnp.sum(p_new, axis=-1, keepdims=True)
        o = _dot_nt(p_old.astype(bf16), v_old_t) + _dot(p_new.astype(bf16), new_rows(c_v))
        return o * (1.0 / l)

    @pl.when(j == 0)
    def _():
        unpack_pages(xs_bufs[0])

    for parity in range(2):
        @pl.when((j > 0) & (j % 2 == parity))
        def _(parity=parity):
            unpack_pages(xs_bufs[parity])
            compress_rows(xs_bufs[1 - parity], j - 1)

    @pl.when(j == nj - 1)
    def _():
        compress_rows(xs_bufs[(steps - 1) % 2], j)
        nseg = past // CMP_STRIDE
        nc = (past + tdec) // CMP_STRIDE - CMP_RATIO + 1
        kc, vc = _cmp_finalize(h_sc, bias_ref, w2p_ref)
        q, trow = queries()
        qpos = past + trow

        n_idx = lax.broadcasted_iota(jnp.int32, (rows, nseg), 1)
        cmask = (n_idx * CMP_STRIDE + (CMP_LEN - 1) <= qpos) & (n_idx < nc)
        p, inv = _softmax_rows(_dot_nt(q, kc.astype(bf16)), cmask)
        p = p * inv
        o_c = _dot(p.astype(bf16), vc.astype(bf16))

        gr = HPG * tdec
        psums = []
        for g in range(KV_GROUPS):
            ps = p[g * gr:g * gr + tdec]
            for h in range(1, HPG):
                ps = ps + p[g * gr + h * tdec:g * gr + (h + 1) * tdec]
            psums.append(ps)
        imp = _importance(jnp.concatenate(psums, axis=0), m_ref)
        n_past_blk = past // SLC_BLOCK
        blk = lax.broadcasted_iota(jnp.int32, imp.shape, 1)
        forced = (blk == 0) | (blk == n_past_blk - 1)
        score = jnp.concatenate([jnp.where(forced, FORCE_SCORE, imp),
                                 jnp.zeros((LANES - imp.shape[0], imp.shape[1]), f32)], axis=0)
        sel_bias = _select_bias(score, TOP_N - 1)
        q_aug = jnp.concatenate(
            [q, jnp.concatenate([sel_bias[g * tdec:(g + 1) * tdec] for g in range(KV_GROUPS)
                                 for _ in range(HPG)], axis=0).astype(bf16)], axis=1)

        o_s = two_part(q, trow, _dot(q_aug, ks_sc[...]), vs_sc[...], 2, 3)

        wb = win_ref.shape[1]
        dist = wb + trow - lax.broadcasted_iota(jnp.int32, (rows, wb), 1)
        s_w = _dot(q, win_ref[0:LANES, :].astype(bf16))
        s_w = jnp.where((dist >= 0) & (dist < WINDOW), s_w, NEG)
        o_w = two_part(q, trow, s_w, win_ref[LANES:2 * LANES, :].astype(bf16), 4, 5)

        gt = gate_ref[...]
        for h in range(nh):
            r0, r1 = h * tdec, (h + 1) * tdec
            c = (h // HPG) * LANES + (h % HPG) * N_GATES
            o = (o_c[r0:r1] * gt[:, c:c + 1] + o_s[r0:r1] * gt[:, c + 1:c + 2]
                 + o_w[r0:r1] * gt[:, c + 2:c + 3])
            o_ref[:, h * LANES:(h + 1) * LANES] = o.astype(o_ref.dtype)


def _attn_sample(page_table, cache, q3, gates3, kv3, win, perm, wz, bias, w2p, imp_m, sel_e):
    bd, n_pages = page_table.shape
    tdec = q3.shape[1]
    past = n_pages * PAGE_SIZE
    npg = PAGES_PER_STEP
    assert n_pages % npg == 0 and past % SLC_BLOCK == 0 and tdec <= CMP_STRIDE
    assert past // SLC_BLOCK == LANES and win.shape[2] <= WINDOW
    nj = n_pages // npg
    nseg = past // CMP_STRIDE
    hw = 2 * KV_GROUPS * CMP_RATIO * CMP_HIDDEN
    assert bd * nj >= PAGE_SLOTS - 1
    per_b = lambda a: pl.BlockSpec((None,) + a.shape[1:], lambda b, j, pt: (b,) + (0,) * (a.ndim - 1))
    full = lambda a: pl.BlockSpec(a.shape, lambda b, j, pt: (0,) * a.ndim)
    grid_spec = pltpu.PrefetchScalarGridSpec(
        num_scalar_prefetch=1,
        grid=(bd, nj),
        in_specs=[pl.BlockSpec(memory_space=pl.ANY), per_b(q3), per_b(gates3), per_b(kv3),
                  per_b(win), full(perm), full(wz), full(bias), full(w2p), full(imp_m), full(sel_e)],
        out_specs=pl.BlockSpec((None, tdec, NSA_HEADS * LANES), lambda b, j, pt: (b, 0, 0)),
        scratch_shapes=[pltpu.VMEM((PAGE_SLOTS * npg, 4 * KVW, PAGE_SIZE), f32),
                        pltpu.SemaphoreType.DMA((PAGE_SLOTS,)),
                        pltpu.VMEM((2, CMP_STRIDE, npg * PAGE_SIZE // CMP_STRIDE, LANES), f32),
                        pltpu.VMEM((2, CMP_STRIDE, npg * PAGE_SIZE // CMP_STRIDE, LANES), f32),
                        pltpu.VMEM((nseg, hw), f32), pltpu.VMEM((2 * LANES, past), bf16),
                        pltpu.VMEM((LANES, past), bf16)],
    )
    return pl.pallas_call(
        functools.partial(_attn_sample_kernel, past=past, tdec=tdec, steps=nj),
        grid_spec=grid_spec,
        out_shape=jax.ShapeDtypeStruct((bd, tdec, NSA_HEADS * LANES), f32),
        compiler_params=pltpu.CompilerParams(dimension_semantics=("arbitrary", "arbitrary"),
                                             vmem_limit_bytes=VMEM_LIMIT),
        name="attn_sample",
    )(page_table.reshape(-1), cache, q3, gates3, kv3, win, perm, wz, bias, w2p, imp_m, sel_e)


def _ffn_rows_kernel(x_ref, a_ref, o_ref, wout_ref, g2_ref, gf_ref, wup_ref, cw_ref, cb_ref, wd_ref,
                     y_ref, tail_ref, carry_sc, *, tiles_per_seq):
    i = pl.program_id(0)
    tm = x_ref.shape[0]
    fc = FFN_FC

    @pl.when(i % tiles_per_seq == 0)
    def _():
        carry_sc[...] = jnp.zeros(carry_sc.shape, f32)

    cat = jnp.concatenate([a_ref[...], o_ref[...]], axis=1)
    hp = x_ref[...] + _dot(cat, wout_ref[...])
    h2 = ((hp * _rms_scale(hp)) * g2_ref[...]).astype(bf16)
    row = lax.broadcasted_iota(jnp.int32, (SUBLANES, fc), 0)

    def shifted(a, prev, k):
        rolled = pltpu.roll(a, k, axis=0)
        top = jnp.where(row < k, pltpu.roll(prev, k, axis=0), rolled[:SUBLANES])
        return jnp.concatenate([top, rolled[SUBLANES:]], axis=0)

    acc = None
    acts = []
    for j in range(D_FF // fc):
        halves = []
        for half in range(2):
            cols = slice(half * D_FF + j * fc, half * D_FF + (j + 1) * fc)
            a = _dot(h2, wup_ref[:, cols])
            tail = a[tm - SUBLANES:, :]
            tail_ref[:, cols] = tail
            prev = carry_sc[:, cols]
            s1 = shifted(a, prev, 1)
            s2 = shifted(a, prev, 2)
            carry_sc[:, cols] = tail
            halves.append(s2 * cw_ref[0:1, cols] + s1 * cw_ref[1:2, cols] + a * cw_ref[2:3, cols]
                          + cb_ref[:, cols])
        acts.append((_gelu(halves[0]) * halves[1]).astype(bf16))
        if len(acts) == FFN_DOWN_GROUP or j == D_FF // fc - 1:
            r1 = (j + 1) * fc
            d = _dot(jnp.concatenate(acts, axis=1), wd_ref[r1 - len(acts) * fc:r1, :])
            acc = d if acc is None else acc + d
            acts = []
    out = hp + acc
    y_ref[...] = (out * _rms_scale(out)) * gf_ref[...]


def _ffn_rows(x, a_out, o_pad, wout, g2, gf, wup, cw, cb, wdn, seq_len):
    n = x.shape[0]
    tm = FFN_TM
    assert n % tm == 0 and seq_len % tm == 0 and D_FF % FFN_FC == 0
    ni = n // tm
    row = lambda w: pl.BlockSpec((tm, w), lambda i: (i, 0))
    resident = lambda a: pl.BlockSpec(a.shape, lambda i: (0,) * a.ndim, pipeline_mode=pl.Buffered(1))
    return pl.pallas_call(
        functools.partial(_ffn_rows_kernel, tiles_per_seq=seq_len // tm),
        grid=(ni,),
        in_specs=[row(D_MODEL), row(GM_WIDTH), row(NSA_HEADS * LANES), resident(wout), resident(g2),
                  resident(gf), resident(wup), resident(cw), resident(cb), resident(wdn)],
        out_specs=[row(D_MODEL), pl.BlockSpec((None, SUBLANES, 2 * D_FF), lambda i: (i, 0, 0))],
        out_shape=[jax.ShapeDtypeStruct((n, D_MODEL), f32),
                   jax.ShapeDtypeStruct((ni, SUBLANES, 2 * D_FF), f32)],
        scratch_shapes=[pltpu.VMEM((SUBLANES, 2 * D_FF), f32)],
        compiler_params=pltpu.CompilerParams(dimension_semantics=("arbitrary",),
                                             vmem_limit_bytes=VMEM_LIMIT),
        name="ffn_prompt",
    )(x, a_out, o_pad, wout, g2, gf, wup, cw, cb, wdn)


def _ffn_kernel(*refs, tiles_per_seq, tdec):
    if tdec:
        (x_ref, a_ref, o_ref, wout_ref, g2_ref, gf_ref, wg_ref, wv_ref, cwg_ref, cwv_ref,
         cbg_ref, cbv_ref, wd_ref, pg_ref, pv_ref, y_ref, tg_ref, tv_ref,
         hp_sc, h2_sc, acc_sc) = refs
    else:
        (x_ref, a_ref, o_ref, wout_ref, g2_ref, gf_ref, wg_ref, wv_ref, cwg_ref, cwv_ref,
         cbg_ref, cbv_ref, wd_ref, y_ref, tg_ref, tv_ref,
         hp_sc, h2_sc, acc_sc, cg_sc, cv_sc) = refs
    i = pl.program_id(0)
    j = pl.program_id(1)
    nj = pl.num_programs(1)
    tm = x_ref.shape[0]

    @pl.when(j == 0)
    def _():
        cat = jnp.concatenate([a_ref[...], o_ref[...].astype(bf16)], axis=1)
        hp = x_ref[...] + _dot(cat, wout_ref[...])
        hp_sc[...] = hp
        h2_sc[...] = ((hp * _rms_scale(hp)) * g2_ref[...]).astype(bf16)
        acc_sc[...] = jnp.zeros(acc_sc.shape, f32)

    h2 = h2_sc[...]
    halves = []
    for w_ref, cw_ref, cb_ref, t_ref, extra in (
            (wg_ref, cwg_ref, cbg_ref, tg_ref, pg_ref if tdec else cg_sc),
            (wv_ref, cwv_ref, cbv_ref, tv_ref, pv_ref if tdec else cv_sc)):
        a = _dot(h2, w_ref[...])
        fc = a.shape[1]
        if tdec:
            nb = tm // tdec
            a3 = a.reshape(nb, tdec, fc)
            t_ref[...] = a3[:, tdec - (CONV_W - 1):, :]
            prev = extra[...]
            trow = lax.broadcasted_iota(jnp.int32, (nb, tdec, fc), 1)
            s1 = jnp.where(trow == 0, prev[:, 1:2, :], pltpu.roll(a3, 1, axis=1))
            s2 = jnp.where(trow == 0, prev[:, 0:1, :],
                           jnp.where(trow == 1, prev[:, 1:2, :], pltpu.roll(a3, 2, axis=1)))
            c = (s2 * cw_ref[0:1, :] + s1 * cw_ref[1:2, :] + a3 * cw_ref[2:3, :]
                 + cb_ref[...]).reshape(tm, fc)
        else:
            tail = a[tm - SUBLANES:, :]
            t_ref[...] = tail

            @pl.when(i % tiles_per_seq == 0)
            def _(extra=extra):
                extra[j] = jnp.zeros(extra.shape[1:], f32)

            prev = extra[j]
            row = lax.broadcasted_iota(jnp.int32, (tm, fc), 0)
            s1 = jnp.where(row == 0, prev[SUBLANES - 1:SUBLANES, :], pltpu.roll(a, 1, axis=0))
            s2 = jnp.where(row == 0, prev[SUBLANES - 2:SUBLANES - 1, :],
                           jnp.where(row == 1, prev[SUBLANES - 1:SUBLANES, :],
                                     pltpu.roll(a, 2, axis=0)))
            extra[j] = tail
            c = s2 * cw_ref[0:1, :] + s1 * cw_ref[1:2, :] + a * cw_ref[2:3, :] + cb_ref[...]
        halves.append(c)
    act = (_gelu(halves[0]) * halves[1]).astype(bf16)
    acc_sc[...] += _dot(act, wd_ref[...])

    @pl.when(j == nj - 1)
    def _():
        out = hp_sc[...] + acc_sc[...]
        y_ref[...] = (out * _rms_scale(out)) * gf_ref[...]


def _ffn(x, a_out, o_pad, wout, g2, gf, wup, cw, cb, wdn, state, seq_len, tdec, tm):
    n = x.shape[0]
    fc = FFN_FC
    assert n % tm == 0 and D_FF % fc == 0
    ni, nj = n // tm, D_FF // fc
    row = lambda w: pl.BlockSpec((tm, w), lambda i, j: (i, 0))
    full = lambda a: pl.BlockSpec(a.shape, lambda i, j: (0,) * a.ndim)
    gate_col = lambda r: pl.BlockSpec((r, fc), lambda i, j: (0, j))
    val_col = lambda r: pl.BlockSpec((r, fc), lambda i, j: (0, nj + j))
    in_specs = [row(D_MODEL), row(GM_WIDTH), row(NSA_HEADS * LANES), full(wout), full(g2), full(gf),
                gate_col(D_MODEL), val_col(D_MODEL), gate_col(CONV_W), val_col(CONV_W),
                gate_col(1), val_col(1), pl.BlockSpec((fc, D_MODEL), lambda i, j: (j, 0))]
    args = [x, a_out, o_pad, wout, g2, gf, wup, wup, cw, cw, cb, cb, wdn]
    scratch = [pltpu.VMEM((tm, D_MODEL), f32), pltpu.VMEM((tm, D_MODEL), bf16),
               pltpu.VMEM((tm, D_MODEL), f32)]
    if tdec:
        assert tm % tdec == 0 and tdec == SUBLANES
        nb = tm // tdec
        in_specs += [pl.BlockSpec((nb, CONV_W - 1, fc), lambda i, j: (i, 0, j)),
                     pl.BlockSpec((nb, CONV_W - 1, fc), lambda i, j: (i, 0, nj + j))]
        args += [state, state]
        tail_shape = jax.ShapeDtypeStruct((n // tdec, CONV_W - 1, D_FF), f32)
        tail_spec = pl.BlockSpec((nb, CONV_W - 1, fc), lambda i, j: (i, 0, j))
        tiles_per_seq = 0
    else:
        assert seq_len % tm == 0
        tiles_per_seq = seq_len // tm
        tail_shape = jax.ShapeDtypeStruct((ni, SUBLANES, D_FF), f32)
        tail_spec = pl.BlockSpec((None, SUBLANES, fc), lambda i, j: (i, 0, j))
        scratch += [pltpu.VMEM((nj, SUBLANES, fc), f32), pltpu.VMEM((nj, SUBLANES, fc), f32)]
    return pl.pallas_call(
        functools.partial(_ffn_kernel, tiles_per_seq=tiles_per_seq, tdec=tdec),
        grid=(ni, nj),
        in_specs=in_specs,
        out_specs=[row(D_MODEL), tail_spec, tail_spec],
        out_shape=[jax.ShapeDtypeStruct((n, D_MODEL), f32), tail_shape, tail_shape],
        scratch_shapes=scratch,
        compiler_params=pltpu.CompilerParams(dimension_semantics=("arbitrary", "arbitrary"),
                                             vmem_limit_bytes=VMEM_LIMIT),
        name="ffn_sample" if tdec else "ffn_prompt",
    )(*args)


def _pad_group_lanes(w, heads_axis_len):
    lead = w.shape[:-1]
    w = w.reshape(lead + (heads_axis_len, HEAD_DIM))
    z = jnp.zeros_like(w)
    first = (jnp.arange(heads_axis_len) < HPG)[:, None]
    lo = jnp.where(first, w, z)
    hi = jnp.where(first, z, w)
    return jnp.concatenate([lo, hi], axis=-1).reshape(lead + (heads_axis_len * LANES,))


def _importance_matrix(nseg, n_blk):
    m = np.zeros((nseg, n_blk), np.float32)
    for n in range(nseg - CMP_RATIO + 1):
        for r in range(CMP_RATIO):
            b = (n + r) // SEG_PER_SLC
            if b < n_blk:
                m[n, b] += 1.0
    return jnp.asarray(m, bf16)


def _block_one_hot(n_blk, n_keys):
    e = (np.arange(n_keys)[None, :] // SLC_BLOCK) == np.arange(n_blk)[:, None]
    return e.astype(np.float32)


def _segment_permutation():
    segs = PAGE_SIZE // CMP_STRIDE
    p = np.zeros((PAGE_SIZE, PAGE_SIZE), np.float32)
    for s in range(CMP_STRIDE):
        for n in range(segs):
            p[s * segs + n, n * CMP_STRIDE + s] = 1.0
    return jnp.asarray(p, bf16)


def kernel(x_prompt, x_sample, cache_kv, cache_win, state_conv, page_table, norm1_g, w_in, gm_ln_g,
           gm_ln_b, gm_ws, gm_bs, cmp_pe, cmp_w1, cmp_w2, w_out, norm2_g, w_up, conv_w, conv_b,
           w_down, final_g):
    depth = w_in.shape[0]
    assert depth == 1
    l = 0
    B, T, _ = x_prompt.shape
    Bd, Td, _ = x_sample.shape
    assert T % CHUNK == 0 and Td == SUBLANES and CHUNK % Td == 0

    wi = w_in[l]
    o_q = 2 * GM_WIDTH
    o_kv = o_q + NSA_WIDTH
    o_gl = o_kv + N_KV_BRANCH * KVW
    wuv = wi[:, :o_q].astype(bf16)
    wq = _pad_group_lanes(wi[:, o_q:o_kv] * (HEAD_DIM ** -0.5), NSA_HEADS).astype(bf16)
    wkv = wi[:, o_kv:o_gl].astype(bf16)
    gpg = HPG * N_GATES
    wgl = jnp.concatenate(
        [jnp.pad(wi[:, o_gl + g * gpg:o_gl + (g + 1) * gpg], ((0, 0), (0, LANES - gpg)))
         for g in range(KV_GROUPS)], axis=1).astype(bf16)
    g1 = norm1_g[l].reshape(1, D_MODEL)
    lng = gm_ln_g[l].reshape(1, GM_WIDTH)
    lnb = gm_ln_b[l].reshape(1, GM_WIDTH)
    causal = jnp.tril(jnp.ones((CHUNK, CHUNK), bool))
    wmix_p = jnp.where(causal, gm_ws[l], 0.0).astype(bf16)
    bmix_p = jnp.broadcast_to(gm_bs[l][:, :, None], (GM_GROUPS, CHUNK, GM_GD))
    reps = CHUNK // Td
    ws_d = jnp.where(causal[:Td, :Td], gm_ws[l][:, :Td, :Td], 0.0)
    eye = jnp.eye(reps, dtype=f32)
    wmix_s = (eye[None, :, None, :, None] * ws_d[:, None, :, None, :]).reshape(
        GM_GROUPS, CHUNK, CHUNK).astype(bf16)
    bmix_s = jnp.broadcast_to(jnp.tile(gm_bs[l][:, :Td], (1, reps))[:, :, None],
                              (GM_GROUPS, CHUNK, GM_GD))

    w1r = cmp_w1[l].reshape(2, CMP_RATIO, CMP_STRIDE, HEAD_DIM, CMP_HIDDEN)
    w1cat = jnp.concatenate([w1r[:, r] for r in range(CMP_RATIO)], axis=-1)
    wz = w1cat.reshape(2, CMP_STRIDE // 4, 4 * HEAD_DIM, CMP_RATIO * CMP_HIDDEN).astype(bf16)
    pe8 = jnp.broadcast_to(cmp_pe[l].reshape(2, 1, CMP_LEN * HEAD_DIM), (2, SUBLANES, CMP_LEN * HEAD_DIM))
    cbias = _cmp_bias(pe8, cmp_w1[l].reshape(2, CMP_LEN * HEAD_DIM, CMP_HIDDEN))
    w2 = cmp_w2[l]
    z2 = jnp.zeros_like(w2)
    w2p = jnp.stack([jnp.concatenate([w2, z2], axis=-1),
                     jnp.concatenate([z2, w2], axis=-1)], axis=1).astype(bf16)

    wo = w_out[l]
    wout = jnp.concatenate([wo[:GM_WIDTH], _pad_group_lanes(wo[GM_WIDTH:].T, NSA_HEADS).T],
                           axis=0).astype(bf16)
    g2 = norm2_g[l].reshape(1, D_MODEL)
    gf = final_g.reshape(1, D_MODEL)
    wup = w_up[l].astype(bf16)
    wdn = w_down[l].astype(bf16)
    cw = conv_w[l]
    cb = conv_b[l].reshape(1, 2 * D_FF)

    xp = x_prompt.reshape(B * T, D_MODEL)
    a_p, _, q_p, kv_p, kvb_p, gate_p = _project(xp, g1, wuv, wq, wkv, wgl, lng, lnb, wmix_p, bmix_p, bf16)
    hid_p = _cmp_hidden(kv_p, wz)
    kc_p, vc_p = _cmp_final(hid_p, cbias, w2p, B)
    imp_m = _importance_matrix(T // CMP_STRIDE, T // SLC_BLOCK)
    one_hot_t = jnp.asarray(_block_one_hot(T // SLC_BLOCK, T).T, bf16)
    o_p = _attn_prompt(q_p, gate_p, kc_p, vc_p, kvb_p, imp_m, one_hot_t, B)
    y_p, tail_p = _ffn_rows(xp, a_p, o_p, wout, g2, gf, wup, cw, cb, wdn, T)
    tiles = T // FFN_TM
    conv_prompt = tail_p.reshape(B, tiles, SUBLANES, 2 * D_FF)[:, tiles - 1, SUBLANES - (CONV_W - 1):]
    kv6 = kv_p.reshape(B, T, N_KV_BRANCH, KV_GROUPS, HEAD_DIM)
    wp = min(WINDOW, T)

    n_pages = page_table.shape[1]
    past = n_pages * PAGE_SIZE
    xs = x_sample.reshape(Bd * Td, D_MODEL)
    a_s, vn_s, q_s, kv_s, _, gate_s = _project(xs, g1, wuv, wq, wkv, wgl, lng, lnb, wmix_s, bmix_s, f32)
    cache = jnp.transpose(cache_kv[l], (0, 2, 3, 4, 1)).reshape(cache_kv.shape[1], 4 * KVW, PAGE_SIZE)
    wb = cache_win.shape[2]
    win = jnp.transpose(cache_win[l], (0, 2, 3, 4, 1)).reshape(Bd, 2 * KVW, wb)
    o_s = _attn_sample(page_table, cache, q_s.reshape(Bd, Td, -1), gate_s.reshape(Bd, Td, -1),
                       kv_s.reshape(Bd, Td, -1), win, _segment_permutation(), wz, cbias, w2p,
                       _importance_matrix(past // CMP_STRIDE, past // SLC_BLOCK),
                       jnp.asarray(_block_one_hot(past // SLC_BLOCK, past), bf16))
    tm_s = min(FFN_TM, Bd * Td)
    y_s, tg_s, tv_s = _ffn(xs, a_s, o_s.reshape(Bd * Td, -1), wout, g2, gf, wup, cw, cb, wdn,
                           state_conv[l], Td, Td, tm_s)
    kv6_s = kv_s.reshape(Bd, Td, N_KV_BRANCH, KV_GROUPS, HEAD_DIM)
    win_new = jnp.concatenate([cache_win[l], kv6_s[:, :, 4:]], axis=1)[:, Td:]

    return (y_p.reshape(B, T, D_MODEL),
            y_s.reshape(Bd, Td, D_MODEL),
            kv6[None, :, :, :4],
            kv6_s[None, :, :, :4],
            kv6[None, :, T - wp:, 4:],
            win_new[None],
            conv_prompt[None],
            jnp.concatenate([tg_s, tv_s], axis=-1)[None],
            vn_s.reshape(1, Bd, Td, GM_WIDTH))
```

```python
import functools
import math

import numpy as np
import jax
import jax.numpy as jnp
from jax import lax
from jax.experimental import pallas as pl
from jax.experimental.pallas import tpu as pltpu

D_MODEL = 1024
GM_WIDTH = D_MODEL // 2
GM_GROUPS = 4
GM_GD = GM_WIDTH // GM_GROUPS
CHUNK = 128
HEAD_DIM = 64
NSA_WIDTH = D_MODEL - GM_WIDTH
NSA_HEADS = NSA_WIDTH // HEAD_DIM
KV_GROUPS = 2
HPG = NSA_HEADS // KV_GROUPS
KVW = KV_GROUPS * HEAD_DIM
N_KV_BRANCH = 6
N_GATES = 3
CMP_LEN = 32
CMP_STRIDE = 16
CMP_RATIO = CMP_LEN // CMP_STRIDE
CMP_HIDDEN = 256
SLC_BLOCK = 64
SLC_SHIFT = SLC_BLOCK.bit_length() - 1
SEG_PER_SLC = SLC_BLOCK // CMP_STRIDE
TOP_N = 16
WINDOW = 512
Q_BLOCK = 128
D_FF = ((8 * D_MODEL // 3 + 127) // 128) * 128
CONV_W = 3
EPS = 1e-6
PAGE_SIZE = 128

LANES = 128
SUBLANES = 8
NEG = -1e30
VMEM_LIMIT = 56 * 1024 * 1024

PROJ_TM = 256
CMP_SEGS = 128
SEL_KC = 512
FFN_TM = 512
FFN_FC = 256
FFN_DOWN_GROUP = 4
PAGES_PER_STEP = 16
PAGE_SLOTS = 4

f32 = jnp.float32
bf16 = jnp.bfloat16

_NT = (((1,), (1,)), ((), ()))


def _gelu(x):
    c = math.sqrt(2.0 / math.pi)
    return x * (0.5 * (1.0 + jnp.tanh(c * (x + 0.044715 * (x * x * x)))))


def _dot(a, b):
    return jnp.dot(a, b, preferred_element_type=f32)


def _dot_nt(a, b):
    return lax.dot_general(a, b, _NT, preferred_element_type=f32)


def _rms_scale(x):
    return lax.rsqrt(jnp.mean(x * x, axis=-1, keepdims=True) + EPS)


def _project_kernel(x_ref, g1_ref, wuv_ref, wq_ref, wkv_ref, wgl_ref, lng_ref, lnb_ref,
                    wmix_ref, bmix_ref, a_ref, vn_ref, q_ref, kv_ref, kvb_ref, gate_ref):
    x = x_ref[...]
    h = ((x * _rms_scale(x)) * g1_ref[...]).astype(bf16)
    uv = _dot(h, wuv_ref[...])
    tm = x.shape[0]
    for g in range(GM_GROUPS):
        lo, hi = g * GM_GD, (g + 1) * GM_GD
        u = _gelu(uv[:, lo:hi])
        v = _gelu(uv[:, GM_WIDTH + lo:GM_WIDTH + hi])
        mu = jnp.mean(v, axis=-1, keepdims=True)
        d = v - mu
        var = jnp.mean(d * d, axis=-1, keepdims=True)
        vn = (d * lax.rsqrt(var + EPS)) * lng_ref[:, lo:hi] + lnb_ref[:, lo:hi]
        vn_ref[:, lo:hi] = vn
        vnb = vn.astype(bf16)
        for c in range(tm // CHUNK):
            r0, r1 = c * CHUNK, (c + 1) * CHUNK
            mixed = _dot(wmix_ref[g], vnb[r0:r1]) + bmix_ref[g]
            a_ref[r0:r1, lo:hi] = (u[r0:r1] * mixed).astype(a_ref.dtype)
    q_ref[...] = _dot(h, wq_ref[...]).astype(q_ref.dtype)
    kv = _dot(h, wkv_ref[...])
    kv_ref[...] = kv
    kvb_ref[...] = kv.astype(bf16)
    gl = _dot(h, wgl_ref[...])
    gate_ref[...] = 1.0 / (1.0 + jnp.exp(-gl))


def _project(x, g1, wuv, wq, wkv, wgl, lng, lnb, wmix, bmix, q_dtype):
    n = x.shape[0]
    tm = PROJ_TM
    assert n % tm == 0
    row = lambda w: pl.BlockSpec((tm, w), lambda i: (i, 0))
    full = lambda a: pl.BlockSpec(a.shape, lambda i: (0,) * a.ndim)
    kvw = N_KV_BRANCH * KVW
    return pl.pallas_call(
        _project_kernel,
        grid=(n // tm,),
        in_specs=[row(D_MODEL), full(g1), full(wuv), full(wq), full(wkv), full(wgl),
                  full(lng), full(lnb), full(wmix), full(bmix)],
        out_specs=[row(GM_WIDTH), row(GM_WIDTH), row(NSA_HEADS * LANES), row(kvw), row(kvw),
                   row(KV_GROUPS * LANES)],
        out_shape=[jax.ShapeDtypeStruct((n, GM_WIDTH), bf16),
                   jax.ShapeDtypeStruct((n, GM_WIDTH), f32),
                   jax.ShapeDtypeStruct((n, NSA_HEADS * LANES), q_dtype),
                   jax.ShapeDtypeStruct((n, kvw), f32),
                   jax.ShapeDtypeStruct((n, kvw), bf16),
                   jax.ShapeDtypeStruct((n, KV_GROUPS * LANES), f32)],
        compiler_params=pltpu.CompilerParams(dimension_semantics=("arbitrary",),
                                             vmem_limit_bytes=VMEM_LIMIT),
        name="project",
    )(x, g1, wuv, wq, wkv, wgl, lng, lnb, wmix, bmix)


def _cmp_hidden_rows(load_rows, w4_ref, kind):
    accs = [None] * KV_GROUPS
    first_half = None
    for s4 in range(CMP_STRIDE // 4):
        pairs = []
        for s in (4 * s4, 4 * s4 + 2):
            a, b = load_rows(s), load_rows(s + 1)
            if first_half is None:
                first_half = lax.broadcasted_iota(jnp.int32, a.shape, 1) < HEAD_DIM
            pairs.append((jnp.where(first_half, a, pltpu.roll(b, HEAD_DIM, axis=1)),
                          jnp.where(first_half, pltpu.roll(a, HEAD_DIM, axis=1), b)))
        for g in range(KV_GROUPS):
            lhs = jnp.concatenate([pairs[0][g], pairs[1][g]], axis=1).astype(bf16)
            d = _dot(lhs, w4_ref[kind, s4])
            accs[g] = d if accs[g] is None else accs[g] + d
    return accs


def _cmp_hidden_kernel(xk_ref, xv_ref, wz_ref, h_ref):
    segs = h_ref.shape[0]
    hw = CMP_RATIO * CMP_HIDDEN
    for kind, x_ref in enumerate((xk_ref, xv_ref)):
        load = lambda s, x_ref=x_ref: x_ref[pl.ds(s, segs, stride=CMP_STRIDE), :]
        accs = _cmp_hidden_rows(load, wz_ref, kind)
        for g in range(KV_GROUPS):
            c0 = (kind * KV_GROUPS + g) * hw
            h_ref[:, c0:c0 + hw] = accs[g]


def _cmp_hidden(kv, wz):
    n = kv.shape[0]
    rows = CMP_SEGS * CMP_STRIDE
    assert n % rows == 0
    hw = 2 * KV_GROUPS * CMP_RATIO * CMP_HIDDEN
    return pl.pallas_call(
        _cmp_hidden_kernel,
        grid=(n // rows,),
        in_specs=[pl.BlockSpec((rows, LANES), lambda i: (i, 0)),
                  pl.BlockSpec((rows, LANES), lambda i: (i, 1)),
                  pl.BlockSpec(wz.shape, lambda i: (0,) * wz.ndim)],
        out_specs=pl.BlockSpec((CMP_SEGS, hw), lambda i: (i, 0)),
        out_shape=jax.ShapeDtypeStruct((n // CMP_STRIDE, hw), f32),
        compiler_params=pltpu.CompilerParams(dimension_semantics=("arbitrary",),
                                             vmem_limit_bytes=VMEM_LIMIT),
        name="cmp_hidden",
    )(kv, kv, wz)


def _cmp_bias_kernel(pe_ref, w1_ref, o_ref):
    for kind in range(2):
        o_ref[kind] = _dot(pe_ref[kind].astype(bf16), w1_ref[kind].astype(bf16))


def _cmp_bias(pe8, w1flat):
    return pl.pallas_call(
        _cmp_bias_kernel,
        out_shape=jax.ShapeDtypeStruct((2, SUBLANES, CMP_HIDDEN), f32),
        name="cmp_bias",
    )(pe8, w1flat)


def _cmp_finalize(h_refs, bias_ref, w2p_ref):
    hw = CMP_RATIO * CMP_HIDDEN
    outs = []
    for kind, h_ref in enumerate(h_refs):
        nseg = h_ref.shape[0]
        acc = None
        for g in range(KV_GROUPS):
            h0 = h_ref[:, g * hw:g * hw + CMP_HIDDEN]
            h1 = h_ref[:, g * hw + CMP_HIDDEN:(g + 1) * hw]
            hid = h0 + pltpu.roll(h1, nseg - 1, axis=0) + bias_ref[kind, 0:1, :]
            d = _dot(_gelu(hid).astype(bf16), w2p_ref[kind, g])
            acc = d if acc is None else acc + d
        outs.append(acc)
    return outs


def _cmp_final_kernel(h_ref, bias_ref, w2p_ref, kc_ref, vc_ref):
    kw = KV_GROUPS * CMP_RATIO * CMP_HIDDEN
    kc, vc = _cmp_finalize((h_ref.at[:, 0:kw], h_ref.at[:, kw:2 * kw]), bias_ref, w2p_ref)
    kc_ref[...] = kc.astype(bf16)
    vc_ref[...] = vc.astype(bf16)


def _cmp_final(hid, bias, w2p, batch):
    nseg = hid.shape[0] // batch
    hw = hid.shape[1]
    full = lambda a: pl.BlockSpec(a.shape, lambda b: (0,) * a.ndim)
    out = jax.ShapeDtypeStruct((batch, nseg, LANES), bf16)
    return pl.pallas_call(
        _cmp_final_kernel,
        grid=(batch,),
        in_specs=[pl.BlockSpec((nseg, hw), lambda b: (b, 0)), full(bias), full(w2p)],
        out_specs=[pl.BlockSpec((None, nseg, LANES), lambda b: (b, 0, 0))] * 2,
        out_shape=[out, out],
        compiler_params=pltpu.CompilerParams(dimension_semantics=("arbitrary",),
                                             vmem_limit_bytes=VMEM_LIMIT),
        name="cmp_final",
    )(hid, bias, w2p)


def _softmax_rows(s, mask):
    s = jnp.where(mask, s, NEG)
    m = jnp.max(s, axis=-1, keepdims=True)
    p = jnp.where(mask, jnp.exp2(s - m), 0.0)
    l = jnp.sum(p, axis=-1, keepdims=True)
    return p, 1.0 / jnp.maximum(l, 1e-30)


def _importance(psum, m):
    hi = psum.astype(bf16)
    lo = (psum - hi.astype(f32)).astype(bf16)
    return _dot(hi, m) + _dot(lo, m)


def _select_bias(score, forced, rounds):
    st = jnp.where(forced, -jnp.inf, score).T
    bias0 = jnp.where(forced, 0.0, NEG).T
    blk = lax.broadcasted_iota(jnp.int32, st.shape, 0).astype(f32)

    def body(_, carry):
        s, bias = carry
        m = jnp.max(s, axis=0, keepdims=True)
        idx = jnp.min(jnp.where(s == m, blk, float(st.shape[0])), axis=0, keepdims=True)
        pick = blk == idx
        return jnp.where(pick, -jnp.inf, s), jnp.where(pick, 0.0, bias)

    _, bias = lax.fori_loop(0, rounds, body, (st, bias0), unroll=True)
    return bias.T


def _with_ones(v):
    return jnp.concatenate([v, jnp.ones(v.shape, v.dtype)], axis=1)


def _attn_prompt_kernel(q_ref, gate_ref, kc_ref, vc_ref, ks_ref, vs_ref, kw_ref, vw_ref,
                        m_ref, e_ref, o_ref, acc_sc, m_sc, s0_sc, s1_sc, p0_sc, p1_sc, a0_sc, a1_sc,
                        *, nc):
    i = pl.program_id(2)
    qb = Q_BLOCK
    q0 = i * qb
    head = lambda h: slice(h * qb, (h + 1) * qb)
    q = jnp.concatenate([q_ref[:, h * LANES:(h + 1) * LANES] for h in range(HPG)], axis=0)
    qpos = q0 + lax.broadcasted_iota(jnp.int32, (qb, 1), 0)

    kc = kc_ref[...]
    nseg = kc.shape[0]
    n_idx = lax.broadcasted_iota(jnp.int32, (qb, nseg), 1)
    cmask = (n_idx * CMP_STRIDE + (CMP_LEN - 1) <= qpos) & (n_idx < nc)
    s_c = _dot_nt(q, kc)
    p_c = []
    psum = None
    for h in range(HPG):
        p, inv = _softmax_rows(s_c[head(h)], cmask)
        p = p * inv
        p_c.append(p.astype(bf16))
        psum = p if psum is None else psum + p
    o_c = _dot(jnp.concatenate(p_c, axis=0), vc_ref[...])

    wlen = WINDOW + qb
    w0 = pl.multiple_of(jnp.maximum(q0 - WINDOW, 0), qb)
    dist = qpos - (w0 + lax.broadcasted_iota(jnp.int32, (qb, wlen), 1))
    wmask = (dist >= 0) & (dist < WINDOW)
    s_w = _dot_nt(q, kw_ref[pl.ds(w0, wlen), :])
    p_w = []
    for h in range(HPG):
        s = jnp.where(wmask, s_w[head(h)], NEG)
        p_w.append(jnp.exp2(s - jnp.max(s, axis=-1, keepdims=True)).astype(bf16))
    o_w = _dot(jnp.concatenate(p_w, axis=0), _with_ones(vw_ref[pl.ds(w0, wlen), :]))

    imp = _importance(psum, m_ref[...])
    n_slc = imp.shape[1]
    blk = lax.broadcasted_iota(jnp.int32, (qb, n_slc), 1)
    cur = jnp.right_shift(q0 + lax.broadcasted_iota(jnp.int32, (qb, n_slc), 0), SLC_SHIFT)
    forced = (blk == 0) | (blk == cur) | (blk == cur - 1)
    score = jnp.where(blk <= cur, imp, -jnp.inf)
    sel_bias = _select_bias(score, forced, TOP_N - 3).astype(bf16)

    q_aug = jnp.concatenate([q, jnp.concatenate([sel_bias] * HPG, axis=0)], axis=1)
    acc_sc[...] = jnp.zeros(acc_sc.shape, f32)
    m_sc[...] = jnp.full(m_sc.shape, NEG, f32)
    kc_len = SEL_KC

    s_bufs = (s0_sc, s1_sc)
    p_bufs = (p0_sc, p1_sc)
    a_bufs = (a0_sc, a1_sc)

    def scores(c, par):
        k0 = pl.multiple_of(c * kc_len, kc_len)
        k_aug = jnp.concatenate([ks_ref[pl.ds(k0, kc_len), :], e_ref[pl.ds(k0, kc_len), :]], axis=1)
        s_bufs[par][...] = _dot_nt(q_aug, k_aug)

    def numerators(c, par, causal):
        if causal:
            k0 = c * kc_len
            ok = k0 + lax.broadcasted_iota(jnp.int32, (qb, kc_len), 1) <= qpos
        for h in range(HPG):
            s = s_bufs[par][head(h), :]
            if causal:
                s = jnp.where(ok, s, NEG)
            m_old = m_sc[head(h), :]
            m_new = jnp.maximum(m_old, jnp.max(s, axis=-1, keepdims=True))
            p_bufs[par][head(h), :] = jnp.exp2(s - m_new).astype(bf16)
            a_bufs[par][head(h), :] = jnp.exp2(m_old - m_new)
            m_sc[head(h), :] = m_new

    def value_sums(c, par):
        k0 = pl.multiple_of(c * kc_len, kc_len)
        pv = _dot(p_bufs[par][...], _with_ones(vs_ref[pl.ds(k0, kc_len), :]))
        acc_sc[...] = a_bufs[par][...] * acc_sc[...] + pv

    def stage(c, par):
        scores(c + 1, 1 - par)
        value_sums(jnp.maximum(c - 1, 0), 1 - par)
        numerators(c, par, False)

    def last_stage(c, par):
        value_sums(jnp.maximum(c - 1, 0), 1 - par)
        numerators(c, par, True)
        value_sums(c, par)

    p_bufs[1][...] = jnp.zeros(p_bufs[1].shape, bf16)
    a_bufs[1][...] = jnp.ones(a_bufs[1].shape, f32)

    def body(k, carry):
        stage(2 * k, 0)
        stage(2 * k + 1, 1)
        return carry

    c_diag = q0 // kc_len
    scores(0, 0)
    lax.fori_loop(0, c_diag // 2, body, 0)

    @pl.when(c_diag % 2 == 0)
    def _():
        last_stage(c_diag, 0)

    @pl.when(c_diag % 2 == 1)
    def _():
        stage(c_diag - 1, 0)
        last_stage(c_diag, 1)

    gt = gate_ref[...]
    for h in range(HPG):
        ow = o_w[head(h)]
        acc = acc_sc[head(h), :]
        c = h * N_GATES
        o = (o_c[head(h)] * gt[:, c:c + 1]
             + acc[:, :LANES] * (1.0 / acc[:, LANES:]) * gt[:, c + 1:c + 2]
             + ow[:, :LANES] * (1.0 / ow[:, LANES:]) * gt[:, c + 2:c + 3])
        o_ref[:, h * LANES:(h + 1) * LANES] = o.astype(o_ref.dtype)


def _attn_prompt(qpad, gates, kc, vc, kvb, imp_m, sel_e, batch):
    n = qpad.shape[0]
    t = n // batch
    qb = Q_BLOCK
    nqb = t // qb
    assert t % SEL_KC == 0 and t >= WINDOW + qb and t // SLC_BLOCK >= TOP_N
    nseg = t // CMP_STRIDE
    nc = nseg - CMP_RATIO + 1
    gw = HPG * LANES
    col = lambda c: pl.BlockSpec((t, LANES), lambda b, g, i, c=c: (b, c))
    cmp_spec = pl.BlockSpec((None, nseg, LANES), lambda b, g, i: (b, 0, 0))
    full = lambda a: pl.BlockSpec(a.shape, lambda b, g, i: (0,) * a.ndim)
    return pl.pallas_call(
        functools.partial(_attn_prompt_kernel, nc=nc),
        grid=(batch, KV_GROUPS, nqb),
        in_specs=[pl.BlockSpec((qb, gw), lambda b, g, i: (b * nqb + i, g)),
                  pl.BlockSpec((qb, LANES), lambda b, g, i: (b * nqb + i, g)),
                  cmp_spec, cmp_spec, col(2), col(3), col(4), col(5),
                  full(imp_m), full(sel_e)],
        out_specs=pl.BlockSpec((qb, gw), lambda b, g, i: (b * nqb + i, g)),
        out_shape=jax.ShapeDtypeStruct((n, NSA_HEADS * LANES), bf16),
        scratch_shapes=[pltpu.VMEM((HPG * qb, 2 * LANES), f32), pltpu.VMEM((HPG * qb, 1), f32),
                        pltpu.VMEM((HPG * qb, SEL_KC), f32), pltpu.VMEM((HPG * qb, SEL_KC), f32),
                        pltpu.VMEM((HPG * qb, SEL_KC), bf16), pltpu.VMEM((HPG * qb, SEL_KC), bf16),
                        pltpu.VMEM((HPG * qb, 1), f32), pltpu.VMEM((HPG * qb, 1), f32)],
        compiler_params=pltpu.CompilerParams(
            dimension_semantics=("arbitrary", "arbitrary", "arbitrary"),
            vmem_limit_bytes=VMEM_LIMIT),
        name="attn_prompt",
    )(qpad, gates, kc, vc, kvb, kvb, kvb, kvb, imp_m, sel_e)


def _attn_sample_kernel(pt_ref, cache_ref, q_ref, gate_ref, kv_ref, win_ref, perm_ref, wz_ref,
                        bias_ref, w2p_ref, m_ref, e_ref, o_ref, pages_sc, page_sem, xs0_sc, xs1_sc,
                        hk_sc, hv_sc, ks_sc, vs_sc, *, past, tdec, steps):
    npg = PAGES_PER_STEP
    j = pl.program_id(1)
    nj = pl.num_programs(1)
    segs_pp = PAGE_SIZE // CMP_STRIDE
    segs = npg * segs_pp
    hw = CMP_RATIO * CMP_HIDDEN
    xs_bufs = (xs0_sc, xs1_sc)
    h_bufs = (hk_sc, hv_sc)

    gstep = pl.program_id(0) * steps + j
    total_steps = pl.num_programs(0) * steps

    def page_copy(step, pi):
        slot = step % PAGE_SLOTS
        return pltpu.make_async_copy(cache_ref.at[pt_ref[step * npg + pi]],
                                     pages_sc.at[slot * npg + pi], page_sem.at[slot])

    def fetch(step):
        for pi in range(npg):
            page_copy(step, pi).start()

    @pl.when(gstep == 0)
    def _():
        for s in range(PAGE_SLOTS - 1):
            fetch(s)

    @pl.when(gstep + (PAGE_SLOTS - 1) < total_steps)
    def _():
        fetch(gstep + (PAGE_SLOTS - 1))

    for pi in range(npg):
        page_copy(gstep, pi).wait()
    slot0 = (gstep % PAGE_SLOTS) * npg
    pages = [pages_sc.at[slot0 + pi] for pi in range(npg)]

    @pl.when((pl.program_id(0) == 0) & (j == 0))
    def _():
        ks_sc[LANES:2 * LANES, :] = e_ref[...]

    def unpack_pages(xs_sc):
        perm = perm_ref[...]
        for pi, pg in enumerate(pages):
            xp = _dot_nt(perm, pg[0:2 * LANES, :].astype(bf16))
            for s in range(CMP_STRIDE):
                rows_s = xp[s * segs_pp:(s + 1) * segs_pp]
                for kind in range(2):
                    xs_sc[kind, s, pi * segs_pp:(pi + 1) * segs_pp, :] = (
                        rows_s[:, kind * LANES:(kind + 1) * LANES])
            c0 = pl.multiple_of((j * npg + pi) * PAGE_SIZE, PAGE_SIZE)
            ks_sc[0:LANES, pl.ds(c0, PAGE_SIZE)] = pg[2 * LANES:3 * LANES, :].astype(bf16)
            vs_sc[:, pl.ds(c0, PAGE_SIZE)] = pg[3 * LANES:4 * LANES, :].astype(bf16)

    def compress_rows(xs_sc, step):
        seg0 = pl.multiple_of(step * segs, segs)
        for kind in range(2):
            load = lambda s, kind=kind: xs_sc[kind, s]
            accs = _cmp_hidden_rows(load, wz_ref, kind)
            for g in range(KV_GROUPS):
                h_bufs[kind][pl.ds(seg0, segs), g * hw:(g + 1) * hw] = accs[g]

    nh = NSA_HEADS
    rows = nh * tdec

    def queries():
        qf = q_ref[...]
        q = jnp.concatenate([qf[:, h * LANES:(h + 1) * LANES] for h in range(nh)],
                            axis=0).astype(bf16)
        return q, lax.broadcasted_iota(jnp.int32, (rows, 1), 0) & (tdec - 1)

    def new_rows(c):
        blk_new = kv_ref[:, c * LANES:(c + 1) * LANES].astype(bf16)
        return jnp.concatenate([blk_new, jnp.zeros((LANES - tdec, LANES), bf16)], axis=0)

    def two_part(q, trow, s_old, v_old_t, c_k, c_v):
        new_mask = lax.broadcasted_iota(jnp.int32, (rows, LANES), 1) <= trow
        s_new = jnp.where(new_mask, _dot_nt(q, new_rows(c_k)), NEG)
        m = jnp.maximum(jnp.max(s_old, axis=-1, keepdims=True),
                        jnp.max(s_new, axis=-1, keepdims=True))
        p_old = jnp.exp2(s_old - m)
        p_new = jnp.exp2(s_new - m)
        l = jnp.sum(p_old, axis=-1, keepdims=True) + jnp.sum(p_new, axis=-1, keepdims=True)
        o = _dot_nt(p_old.astype(bf16), v_old_t) + _dot(p_new.astype(bf16), new_rows(c_v))
        return o * (1.0 / l)

    @pl.when(j == 0)
    def _():
        unpack_pages(xs_bufs[0])

    for parity in range(2):
        @pl.when((j > 0) & (j % 2 == parity))
        def _(parity=parity):
            unpack_pages(xs_bufs[parity])
            compress_rows(xs_bufs[1 - parity], j - 1)

    @pl.when(j == nj - 1)
    def _():
        compress_rows(xs_bufs[(steps - 1) % 2], j)
        nseg = past // CMP_STRIDE
        nc = (past + tdec) // CMP_STRIDE - CMP_RATIO + 1
        kc, vc = _cmp_finalize(h_bufs, bias_ref, w2p_ref)
        q, trow = queries()
        qpos = past + trow

        n_idx = lax.broadcasted_iota(jnp.int32, (rows, nseg), 1)
        cmask = (n_idx * CMP_STRIDE + (CMP_LEN - 1) <= qpos) & (n_idx < nc)
        p, inv = _softmax_rows(_dot_nt(q, kc.astype(bf16)), cmask)
        p = p * inv
        o_c = _dot(p.astype(bf16), vc.astype(bf16))

        gr = HPG * tdec
        psums = []
        for g in range(KV_GROUPS):
            ps = p[g * gr:g * gr + tdec]
            for h in range(1, HPG):
                ps = ps + p[g * gr + h * tdec:g * gr + (h + 1) * tdec]
            psums.append(ps)
        imp = _importance(jnp.concatenate(psums, axis=0), m_ref[...])
        n_past_blk = past // SLC_BLOCK
        blk = lax.broadcasted_iota(jnp.int32, imp.shape, 1)
        forced = (blk == 0) | (blk == n_past_blk - 1)
        pad = jnp.zeros((LANES - imp.shape[0], imp.shape[1]), f32)
        sel_bias = _select_bias(jnp.concatenate([imp, pad], axis=0),
                                jnp.concatenate([forced, pad > 0], axis=0), TOP_N - 3)
        q_aug = jnp.concatenate(
            [q, jnp.concatenate([sel_bias[g * tdec:(g + 1) * tdec] for g in range(KV_GROUPS)
                                 for _ in range(HPG)], axis=0).astype(bf16)], axis=1)

        o_s = two_part(q, trow, _dot(q_aug, ks_sc[...]), vs_sc[...], 2, 3)

        wb = win_ref.shape[1]
        dist = wb + trow - lax.broadcasted_iota(jnp.int32, (rows, wb), 1)
        s_w = _dot(q, win_ref[0:LANES, :].astype(bf16))
        s_w = jnp.where((dist >= 0) & (dist < WINDOW), s_w, NEG)
        o_w = two_part(q, trow, s_w, win_ref[LANES:2 * LANES, :].astype(bf16), 4, 5)

        gt = gate_ref[...]
        for h in range(nh):
            r0, r1 = h * tdec, (h + 1) * tdec
            c = (h // HPG) * LANES + (h % HPG) * N_GATES
            o = (o_c[r0:r1] * gt[:, c:c + 1] + o_s[r0:r1] * gt[:, c + 1:c + 2]
                 + o_w[r0:r1] * gt[:, c + 2:c + 3])
            o_ref[:, h * LANES:(h + 1) * LANES] = o.astype(o_ref.dtype)


def _attn_sample(page_table, cache, q3, gates3, kv3, win, perm, wz, bias, w2p, imp_m, sel_e):
    bd, n_pages = page_table.shape
    tdec = q3.shape[1]
    past = n_pages * PAGE_SIZE
    npg = PAGES_PER_STEP
    assert n_pages % npg == 0 and past % SLC_BLOCK == 0 and tdec <= CMP_STRIDE
    assert past // SLC_BLOCK == LANES and win.shape[2] <= WINDOW
    nj = n_pages // npg
    nseg = past // CMP_STRIDE
    hw = 2 * KV_GROUPS * CMP_RATIO * CMP_HIDDEN
    assert bd * nj >= PAGE_SLOTS - 1
    per_b = lambda a: pl.BlockSpec((None,) + a.shape[1:], lambda b, j, pt: (b,) + (0,) * (a.ndim - 1))
    full = lambda a: pl.BlockSpec(a.shape, lambda b, j, pt: (0,) * a.ndim)
    grid_spec = pltpu.PrefetchScalarGridSpec(
        num_scalar_prefetch=1,
        grid=(bd, nj),
        in_specs=[pl.BlockSpec(memory_space=pl.ANY), per_b(q3), per_b(gates3), per_b(kv3),
                  per_b(win), full(perm), full(wz), full(bias), full(w2p), full(imp_m), full(sel_e)],
        out_specs=pl.BlockSpec((None, tdec, NSA_HEADS * LANES), lambda b, j, pt: (b, 0, 0)),
        scratch_shapes=[pltpu.VMEM((PAGE_SLOTS * npg, 4 * KVW, PAGE_SIZE), f32),
                        pltpu.SemaphoreType.DMA((PAGE_SLOTS,)),
                        pltpu.VMEM((2, CMP_STRIDE, npg * PAGE_SIZE // CMP_STRIDE, LANES), f32),
                        pltpu.VMEM((2, CMP_STRIDE, npg * PAGE_SIZE // CMP_STRIDE, LANES), f32),
                        pltpu.VMEM((nseg, hw // 2), f32), pltpu.VMEM((nseg, hw // 2), f32),
                        pltpu.VMEM((2 * LANES, past), bf16),
                        pltpu.VMEM((LANES, past), bf16)],
    )
    return pl.pallas_call(
        functools.partial(_attn_sample_kernel, past=past, tdec=tdec, steps=nj),
        grid_spec=grid_spec,
        out_shape=jax.ShapeDtypeStruct((bd, tdec, NSA_HEADS * LANES), f32),
        compiler_params=pltpu.CompilerParams(dimension_semantics=("arbitrary", "arbitrary"),
                                             vmem_limit_bytes=VMEM_LIMIT),
        name="attn_sample",
    )(page_table.reshape(-1), cache, q3, gates3, kv3, win, perm, wz, bias, w2p, imp_m, sel_e)


def _ffn_rows_kernel(x_ref, a_ref, o_ref, wout_ref, g2_ref, gf_ref, wup_ref, cw_ref, cb_ref, wd_ref,
                     y_ref, tail_ref, carry_sc, *, tiles_per_seq):
    i = pl.program_id(0)
    tm = x_ref.shape[0]
    fc = FFN_FC

    @pl.when(i % tiles_per_seq == 0)
    def _():
        carry_sc[...] = jnp.zeros(carry_sc.shape, f32)

    cat = jnp.concatenate([a_ref[...], o_ref[...]], axis=1)
    hp = x_ref[...] + _dot(cat, wout_ref[...])
    h2 = ((hp * _rms_scale(hp)) * g2_ref[...]).astype(bf16)
    row = lax.broadcasted_iota(jnp.int32, (SUBLANES, fc), 0)

    def shifted(a, prev, k):
        rolled = pltpu.roll(a, k, axis=0)
        top = jnp.where(row < k, pltpu.roll(prev, k, axis=0), rolled[:SUBLANES])
        return jnp.concatenate([top, rolled[SUBLANES:]], axis=0)

    acc = None
    acts = []
    for j in range(D_FF // fc):
        halves = []
        for half in range(2):
            cols = slice(half * D_FF + j * fc, half * D_FF + (j + 1) * fc)
            a = _dot(h2, wup_ref[:, cols])
            tail = a[tm - SUBLANES:, :]
            tail_ref[:, cols] = tail
            prev = carry_sc[:, cols]
            s1 = shifted(a, prev, 1)
            s2 = shifted(a, prev, 2)
            carry_sc[:, cols] = tail
            halves.append(s2 * cw_ref[0:1, cols] + s1 * cw_ref[1:2, cols] + a * cw_ref[2:3, cols]
                          + cb_ref[:, cols])
        acts.append((_gelu(halves[0]) * halves[1]).astype(bf16))
        if len(acts) == FFN_DOWN_GROUP or j == D_FF // fc - 1:
            r1 = (j + 1) * fc
            d = _dot(jnp.concatenate(acts, axis=1), wd_ref[r1 - len(acts) * fc:r1, :])
            acc = d if acc is None else acc + d
            acts = []
    out = hp + acc
    y_ref[...] = (out * _rms_scale(out)) * gf_ref[...]


def _ffn_rows(x, a_out, o_pad, wout, g2, gf, wup, cw, cb, wdn, seq_len):
    n = x.shape[0]
    tm = FFN_TM
    assert n % tm == 0 and seq_len % tm == 0 and D_FF % FFN_FC == 0
    ni = n // tm
    row = lambda w: pl.BlockSpec((tm, w), lambda i: (i, 0))
    resident = lambda a: pl.BlockSpec(a.shape, lambda i: (0,) * a.ndim, pipeline_mode=pl.Buffered(1))
    return pl.pallas_call(
        functools.partial(_ffn_rows_kernel, tiles_per_seq=seq_len // tm),
        grid=(ni,),
        in_specs=[row(D_MODEL), row(GM_WIDTH), row(NSA_HEADS * LANES), resident(wout), resident(g2),
                  resident(gf), resident(wup), resident(cw), resident(cb), resident(wdn)],
        out_specs=[row(D_MODEL), pl.BlockSpec((None, SUBLANES, 2 * D_FF), lambda i: (i, 0, 0))],
        out_shape=[jax.ShapeDtypeStruct((n, D_MODEL), f32),
                   jax.ShapeDtypeStruct((ni, SUBLANES, 2 * D_FF), f32)],
        scratch_shapes=[pltpu.VMEM((SUBLANES, 2 * D_FF), f32)],
        compiler_params=pltpu.CompilerParams(dimension_semantics=("arbitrary",),
                                             vmem_limit_bytes=VMEM_LIMIT),
        name="ffn_prompt",
    )(x, a_out, o_pad, wout, g2, gf, wup, cw, cb, wdn)


def _ffn_kernel(*refs, tiles_per_seq, tdec):
    if tdec:
        (x_ref, a_ref, o_ref, wout_ref, g2_ref, gf_ref, wg_ref, wv_ref, cwg_ref, cwv_ref,
         cbg_ref, cbv_ref, wd_ref, pg_ref, pv_ref, y_ref, tg_ref, tv_ref,
         hp_sc, h2_sc, acc_sc) = refs
    else:
        (x_ref, a_ref, o_ref, wout_ref, g2_ref, gf_ref, wg_ref, wv_ref, cwg_ref, cwv_ref,
         cbg_ref, cbv_ref, wd_ref, y_ref, tg_ref, tv_ref,
         hp_sc, h2_sc, acc_sc, cg_sc, cv_sc) = refs
    i = pl.program_id(0)
    j = pl.program_id(1)
    nj = pl.num_programs(1)
    tm = x_ref.shape[0]

    @pl.when(j == 0)
    def _():
        cat = jnp.concatenate([a_ref[...], o_ref[...].astype(bf16)], axis=1)
        hp = x_ref[...] + _dot(cat, wout_ref[...])
        hp_sc[...] = hp
        h2_sc[...] = ((hp * _rms_scale(hp)) * g2_ref[...]).astype(bf16)
        acc_sc[...] = jnp.zeros(acc_sc.shape, f32)

    h2 = h2_sc[...]
    halves = []
    for w_ref, cw_ref, cb_ref, t_ref, extra in (
            (wg_ref, cwg_ref, cbg_ref, tg_ref, pg_ref if tdec else cg_sc),
            (wv_ref, cwv_ref, cbv_ref, tv_ref, pv_ref if tdec else cv_sc)):
        a = _dot(h2, w_ref[...])
        fc = a.shape[1]
        if tdec:
            nb = tm // tdec
            a3 = a.reshape(nb, tdec, fc)
            t_ref[...] = a3[:, tdec - (CONV_W - 1):, :]
            prev = extra[...]
            trow = lax.broadcasted_iota(jnp.int32, (nb, tdec, fc), 1)
            s1 = jnp.where(trow == 0, prev[:, 1:2, :], pltpu.roll(a3, 1, axis=1))
            s2 = jnp.where(trow == 0, prev[:, 0:1, :],
                           jnp.where(trow == 1, prev[:, 1:2, :], pltpu.roll(a3, 2, axis=1)))
            c = (s2 * cw_ref[0:1, :] + s1 * cw_ref[1:2, :] + a3 * cw_ref[2:3, :]
                 + cb_ref[...]).reshape(tm, fc)
        else:
            tail = a[tm - SUBLANES:, :]
            t_ref[...] = tail

            @pl.when(i % tiles_per_seq == 0)
            def _(extra=extra):
                extra[j] = jnp.zeros(extra.shape[1:], f32)

            prev = extra[j]
            row = lax.broadcasted_iota(jnp.int32, (tm, fc), 0)
            s1 = jnp.where(row == 0, prev[SUBLANES - 1:SUBLANES, :], pltpu.roll(a, 1, axis=0))
            s2 = jnp.where(row == 0, prev[SUBLANES - 2:SUBLANES - 1, :],
                           jnp.where(row == 1, prev[SUBLANES - 1:SUBLANES, :],
                                     pltpu.roll(a, 2, axis=0)))
            extra[j] = tail
            c = s2 * cw_ref[0:1, :] + s1 * cw_ref[1:2, :] + a * cw_ref[2:3, :] + cb_ref[...]
        halves.append(c)
    act = (_gelu(halves[0]) * halves[1]).astype(bf16)
    acc_sc[...] += _dot(act, wd_ref[...])

    @pl.when(j == nj - 1)
    def _():
        out = hp_sc[...] + acc_sc[...]
        y_ref[...] = (out * _rms_scale(out)) * gf_ref[...]


def _ffn(x, a_out, o_pad, wout, g2, gf, wup, cw, cb, wdn, state, seq_len, tdec, tm):
    n = x.shape[0]
    fc = FFN_FC
    assert n % tm == 0 and D_FF % fc == 0
    ni, nj = n // tm, D_FF // fc
    row = lambda w: pl.BlockSpec((tm, w), lambda i, j: (i, 0))
    full = lambda a: pl.BlockSpec(a.shape, lambda i, j: (0,) * a.ndim)
    gate_col = lambda r: pl.BlockSpec((r, fc), lambda i, j: (0, j))
    val_col = lambda r: pl.BlockSpec((r, fc), lambda i, j: (0, nj + j))
    in_specs = [row(D_MODEL), row(GM_WIDTH), row(NSA_HEADS * LANES), full(wout), full(g2), full(gf),
                gate_col(D_MODEL), val_col(D_MODEL), gate_col(CONV_W), val_col(CONV_W),
                gate_col(1), val_col(1), pl.BlockSpec((fc, D_MODEL), lambda i, j: (j, 0))]
    args = [x, a_out, o_pad, wout, g2, gf, wup, wup, cw, cw, cb, cb, wdn]
    scratch = [pltpu.VMEM((tm, D_MODEL), f32), pltpu.VMEM((tm, D_MODEL), bf16),
               pltpu.VMEM((tm, D_MODEL), f32)]
    if tdec:
        assert tm % tdec == 0 and tdec == SUBLANES
        nb = tm // tdec
        in_specs += [pl.BlockSpec((nb, CONV_W - 1, fc), lambda i, j: (i, 0, j)),
                     pl.BlockSpec((nb, CONV_W - 1, fc), lambda i, j: (i, 0, nj + j))]
        args += [state, state]
        tail_shape = jax.ShapeDtypeStruct((n // tdec, CONV_W - 1, D_FF), f32)
        tail_spec = pl.BlockSpec((nb, CONV_W - 1, fc), lambda i, j: (i, 0, j))
        tiles_per_seq = 0
    else:
        assert seq_len % tm == 0
        tiles_per_seq = seq_len // tm
        tail_shape = jax.ShapeDtypeStruct((ni, SUBLANES, D_FF), f32)
        tail_spec = pl.BlockSpec((None, SUBLANES, fc), lambda i, j: (i, 0, j))
        scratch += [pltpu.VMEM((nj, SUBLANES, fc), f32), pltpu.VMEM((nj, SUBLANES, fc), f32)]
    return pl.pallas_call(
        functools.partial(_ffn_kernel, tiles_per_seq=tiles_per_seq, tdec=tdec),
        grid=(ni, nj),
        in_specs=in_specs,
        out_specs=[row(D_MODEL), tail_spec, tail_spec],
        out_shape=[jax.ShapeDtypeStruct((n, D_MODEL), f32), tail_shape, tail_shape],
        scratch_shapes=scratch,
        compiler_params=pltpu.CompilerParams(dimension_semantics=("arbitrary", "arbitrary"),
                                             vmem_limit_bytes=VMEM_LIMIT),
        name="ffn_sample" if tdec else "ffn_prompt",
    )(*args)


def _pad_group_lanes(w, heads_axis_len):
    lead = w.shape[:-1]
    w = w.reshape(lead + (heads_axis_len, HEAD_DIM))
    z = jnp.zeros_like(w)
    first = (jnp.arange(heads_axis_len) < HPG)[:, None]
    lo = jnp.where(first, w, z)
    hi = jnp.where(first, z, w)
    return jnp.concatenate([lo, hi], axis=-1).reshape(lead + (heads_axis_len * LANES,))


def _importance_matrix(nseg, n_blk):
    m = np.zeros((nseg, n_blk), np.float32)
    for n in range(nseg - CMP_RATIO + 1):
        for r in range(CMP_RATIO):
            b = (n + r) // SEG_PER_SLC
            if b < n_blk:
                m[n, b] += 1.0
    return jnp.asarray(m, bf16)


def _block_one_hot(n_blk, n_keys):
    e = (np.arange(n_keys)[None, :] // SLC_BLOCK) == np.arange(n_blk)[:, None]
    return e.astype(np.float32)


def _segment_permutation():
    segs = PAGE_SIZE // CMP_STRIDE
    p = np.zeros((PAGE_SIZE, PAGE_SIZE), np.float32)
    for s in range(CMP_STRIDE):
        for n in range(segs):
            p[s * segs + n, n * CMP_STRIDE + s] = 1.0
    return jnp.asarray(p, bf16)


def kernel(x_prompt, x_sample, cache_kv, cache_win, state_conv, page_table, norm1_g, w_in, gm_ln_g,
           gm_ln_b, gm_ws, gm_bs, cmp_pe, cmp_w1, cmp_w2, w_out, norm2_g, w_up, conv_w, conv_b,
           w_down, final_g):
    depth = w_in.shape[0]
    assert depth == 1
    l = 0
    B, T, _ = x_prompt.shape
    Bd, Td, _ = x_sample.shape
    assert T % CHUNK == 0 and Td == SUBLANES and CHUNK % Td == 0

    wi = w_in[l]
    o_q = 2 * GM_WIDTH
    o_kv = o_q + NSA_WIDTH
    o_gl = o_kv + N_KV_BRANCH * KVW
    wuv = wi[:, :o_q].astype(bf16)
    wq = _pad_group_lanes(wi[:, o_q:o_kv] * (HEAD_DIM ** -0.5 * math.log2(math.e)),
                          NSA_HEADS).astype(bf16)
    wkv = wi[:, o_kv:o_gl].astype(bf16)
    gpg = HPG * N_GATES
    wgl = jnp.concatenate(
        [jnp.pad(wi[:, o_gl + g * gpg:o_gl + (g + 1) * gpg], ((0, 0), (0, LANES - gpg)))
         for g in range(KV_GROUPS)], axis=1).astype(bf16)
    g1 = norm1_g[l].reshape(1, D_MODEL)
    lng = gm_ln_g[l].reshape(1, GM_WIDTH)
    lnb = gm_ln_b[l].reshape(1, GM_WIDTH)
    causal = jnp.tril(jnp.ones((CHUNK, CHUNK), bool))
    wmix_p = jnp.where(causal, gm_ws[l], 0.0).astype(bf16)
    bmix_p = jnp.broadcast_to(gm_bs[l][:, :, None], (GM_GROUPS, CHUNK, GM_GD))
    reps = CHUNK // Td
    ws_d = jnp.where(causal[:Td, :Td], gm_ws[l][:, :Td, :Td], 0.0)
    eye = jnp.eye(reps, dtype=f32)
    wmix_s = (eye[None, :, None, :, None] * ws_d[:, None, :, None, :]).reshape(
        GM_GROUPS, CHUNK, CHUNK).astype(bf16)
    bmix_s = jnp.broadcast_to(jnp.tile(gm_bs[l][:, :Td], (1, reps))[:, :, None],
                              (GM_GROUPS, CHUNK, GM_GD))

    w1r = cmp_w1[l].reshape(2, CMP_RATIO, CMP_STRIDE, HEAD_DIM, CMP_HIDDEN)
    w1cat = jnp.concatenate([w1r[:, r] for r in range(CMP_RATIO)], axis=-1)
    wz = w1cat.reshape(2, CMP_STRIDE // 4, 4 * HEAD_DIM, CMP_RATIO * CMP_HIDDEN).astype(bf16)
    pe8 = jnp.broadcast_to(cmp_pe[l].reshape(2, 1, CMP_LEN * HEAD_DIM), (2, SUBLANES, CMP_LEN * HEAD_DIM))
    cbias = _cmp_bias(pe8, cmp_w1[l].reshape(2, CMP_LEN * HEAD_DIM, CMP_HIDDEN))
    w2 = cmp_w2[l]
    z2 = jnp.zeros_like(w2)
    w2p = jnp.stack([jnp.concatenate([w2, z2], axis=-1),
                     jnp.concatenate([z2, w2], axis=-1)], axis=1).astype(bf16)

    wo = w_out[l]
    wout = jnp.concatenate([wo[:GM_WIDTH], _pad_group_lanes(wo[GM_WIDTH:].T, NSA_HEADS).T],
                           axis=0).astype(bf16)
    g2 = norm2_g[l].reshape(1, D_MODEL)
    gf = final_g.reshape(1, D_MODEL)
    wup = w_up[l].astype(bf16)
    wdn = w_down[l].astype(bf16)
    cw = conv_w[l]
    cb = conv_b[l].reshape(1, 2 * D_FF)

    xp = x_prompt.reshape(B * T, D_MODEL)
    a_p, _, q_p, kv_p, kvb_p, gate_p = _project(xp, g1, wuv, wq, wkv, wgl, lng, lnb, wmix_p, bmix_p, bf16)
    hid_p = _cmp_hidden(kv_p, wz)
    kc_p, vc_p = _cmp_final(hid_p, cbias, w2p, B)
    imp_m = _importance_matrix(T // CMP_STRIDE, T // SLC_BLOCK)
    one_hot_t = jnp.asarray(_block_one_hot(T // SLC_BLOCK, T).T, bf16)
    o_p = _attn_prompt(q_p, gate_p, kc_p, vc_p, kvb_p, imp_m, one_hot_t, B)
    y_p, tail_p = _ffn_rows(xp, a_p, o_p, wout, g2, gf, wup, cw, cb, wdn, T)
    tiles = T // FFN_TM
    conv_prompt = tail_p.reshape(B, tiles, SUBLANES, 2 * D_FF)[:, tiles - 1, SUBLANES - (CONV_W - 1):]
    kv6 = kv_p.reshape(B, T, N_KV_BRANCH, KV_GROUPS, HEAD_DIM)
    wp = min(WINDOW, T)

    n_pages = page_table.shape[1]
    past = n_pages * PAGE_SIZE
    xs = x_sample.reshape(Bd * Td, D_MODEL)
    a_s, vn_s, q_s, kv_s, _, gate_s = _project(xs, g1, wuv, wq, wkv, wgl, lng, lnb, wmix_s, bmix_s, f32)
    cache = jnp.transpose(cache_kv[l], (0, 2, 3, 4, 1)).reshape(cache_kv.shape[1], 4 * KVW, PAGE_SIZE)
    wb = cache_win.shape[2]
    win = jnp.transpose(cache_win[l], (0, 2, 3, 4, 1)).reshape(Bd, 2 * KVW, wb)
    o_s = _attn_sample(page_table, cache, q_s.reshape(Bd, Td, -1), gate_s.reshape(Bd, Td, -1),
                       kv_s.reshape(Bd, Td, -1), win, _segment_permutation(), wz, cbias, w2p,
                       _importance_matrix(past // CMP_STRIDE, past // SLC_BLOCK),
                       jnp.asarray(_block_one_hot(past // SLC_BLOCK, past), bf16))
    tm_s = min(FFN_TM, Bd * Td)
    y_s, tg_s, tv_s = _ffn(xs, a_s, o_s.reshape(Bd * Td, -1), wout, g2, gf, wup, cw, cb, wdn,
                           state_conv[l], Td, Td, tm_s)
    kv6_s = kv_s.reshape(Bd, Td, N_KV_BRANCH, KV_GROUPS, HEAD_DIM)
    win_new = jnp.concatenate([cache_win[l], kv6_s[:, :, 4:]], axis=1)[:, Td:]

    return (y_p.reshape(B, T, D_MODEL),
            y_s.reshape(Bd, Td, D_MODEL),
            kv6[None, :, :, :4],
            kv6_s[None, :, :, :4],
            kv6[None, :, T - wp:, 4:],
            win_new[None],
            conv_prompt[None],
            jnp.concatenate([tg_s, tv_s], axis=-1)[None],
            vn_s.reshape(1, Bd, Td, GM_WIDTH))
```

```python
import functools
import math

import numpy as np
import jax
import jax.numpy as jnp
from jax import lax
from jax.experimental import pallas as pl
from jax.experimental.pallas import tpu as pltpu

D_MODEL = 1024
GM_WIDTH = D_MODEL // 2
GM_GROUPS = 4
GM_GD = GM_WIDTH // GM_GROUPS
CHUNK = 128
HEAD_DIM = 64
NSA_WIDTH = D_MODEL - GM_WIDTH
NSA_HEADS = NSA_WIDTH // HEAD_DIM
KV_GROUPS = 2
HPG = NSA_HEADS // KV_GROUPS
KVW = KV_GROUPS * HEAD_DIM
N_KV_BRANCH = 6
N_GATES = 3
CMP_LEN = 32
CMP_STRIDE = 16
CMP_RATIO = CMP_LEN // CMP_STRIDE
CMP_HIDDEN = 256
SLC_BLOCK = 64
SLC_SHIFT = SLC_BLOCK.bit_length() - 1
SEG_PER_SLC = SLC_BLOCK // CMP_STRIDE
TOP_N = 16
WINDOW = 512
Q_BLOCK = 128
D_FF = ((8 * D_MODEL // 3 + 127) // 128) * 128
CONV_W = 3
EPS = 1e-6
PAGE_SIZE = 128

LANES = 128
SUBLANES = 8
NEG = -1e30
VMEM_LIMIT = 56 * 1024 * 1024

PROJ_TM = 256
CMP_SEGS = 128
SEL_KC = 512
FFN_TM = 512
FFN_FC = 256
FFN_DOWN_GROUP = 4
PAGES_PER_STEP = 16
PAGE_SLOTS = 4

f32 = jnp.float32
bf16 = jnp.bfloat16

_NT = (((1,), (1,)), ((), ()))


def _gelu(x):
    c = math.sqrt(2.0 / math.pi)
    return x * (0.5 * (1.0 + jnp.tanh(c * (x + 0.044715 * (x * x * x)))))


def _dot(a, b):
    return jnp.dot(a, b, preferred_element_type=f32)


def _dot_nt(a, b):
    return lax.dot_general(a, b, _NT, preferred_element_type=f32)


def _rms_scale(x):
    return lax.rsqrt(jnp.mean(x * x, axis=-1, keepdims=True) + EPS)


def _project_kernel(x_ref, g1_ref, wuv_ref, wq_ref, wkv_ref, wgl_ref, lng_ref, lnb_ref,
                    wmix_ref, bmix_ref, a_ref, vn_ref, q_ref, kv_ref, kvb_ref, gate_ref):
    x = x_ref[...]
    h = ((x * _rms_scale(x)) * g1_ref[...]).astype(bf16)
    uv = _dot(h, wuv_ref[...])
    tm = x.shape[0]
    for g in range(GM_GROUPS):
        lo, hi = g * GM_GD, (g + 1) * GM_GD
        u = _gelu(uv[:, lo:hi])
        v = _gelu(uv[:, GM_WIDTH + lo:GM_WIDTH + hi])
        mu = jnp.mean(v, axis=-1, keepdims=True)
        d = v - mu
        var = jnp.mean(d * d, axis=-1, keepdims=True)
        vn = (d * lax.rsqrt(var + EPS)) * lng_ref[:, lo:hi] + lnb_ref[:, lo:hi]
        vn_ref[:, lo:hi] = vn
        vnb = vn.astype(bf16)
        for c in range(tm // CHUNK):
            r0, r1 = c * CHUNK, (c + 1) * CHUNK
            mixed = _dot(wmix_ref[g], vnb[r0:r1]) + bmix_ref[g]
            a_ref[r0:r1, lo:hi] = (u[r0:r1] * mixed).astype(a_ref.dtype)
    q_ref[...] = _dot(h, wq_ref[...]).astype(q_ref.dtype)
    kv = _dot(h, wkv_ref[...])
    kv_ref[...] = kv
    kvb_ref[...] = kv.astype(bf16)
    gl = _dot(h, wgl_ref[...])
    gate_ref[...] = 1.0 / (1.0 + jnp.exp(-gl))


def _project(x, g1, wuv, wq, wkv, wgl, lng, lnb, wmix, bmix, q_dtype):
    n = x.shape[0]
    tm = PROJ_TM
    assert n % tm == 0
    row = lambda w: pl.BlockSpec((tm, w), lambda i: (i, 0))
    full = lambda a: pl.BlockSpec(a.shape, lambda i: (0,) * a.ndim)
    kvw = N_KV_BRANCH * KVW
    return pl.pallas_call(
        _project_kernel,
        grid=(n // tm,),
        in_specs=[row(D_MODEL), full(g1), full(wuv), full(wq), full(wkv), full(wgl),
                  full(lng), full(lnb), full(wmix), full(bmix)],
        out_specs=[row(GM_WIDTH), row(GM_WIDTH), row(NSA_HEADS * LANES), row(kvw), row(kvw),
                   row(KV_GROUPS * LANES)],
        out_shape=[jax.ShapeDtypeStruct((n, GM_WIDTH), bf16),
                   jax.ShapeDtypeStruct((n, GM_WIDTH), f32),
                   jax.ShapeDtypeStruct((n, NSA_HEADS * LANES), q_dtype),
                   jax.ShapeDtypeStruct((n, kvw), f32),
                   jax.ShapeDtypeStruct((n, kvw), bf16),
                   jax.ShapeDtypeStruct((n, KV_GROUPS * LANES), f32)],
        compiler_params=pltpu.CompilerParams(dimension_semantics=("arbitrary",),
                                             vmem_limit_bytes=VMEM_LIMIT),
        name="project",
    )(x, g1, wuv, wq, wkv, wgl, lng, lnb, wmix, bmix)


def _cmp_hidden_rows(load_rows, w4_ref, kind):
    accs = [None] * KV_GROUPS
    first_half = None
    for s4 in range(CMP_STRIDE // 4):
        pairs = []
        for s in (4 * s4, 4 * s4 + 2):
            a, b = load_rows(s), load_rows(s + 1)
            if first_half is None:
                first_half = lax.broadcasted_iota(jnp.int32, a.shape, 1) < HEAD_DIM
            pairs.append((jnp.where(first_half, a, pltpu.roll(b, HEAD_DIM, axis=1)),
                          jnp.where(first_half, pltpu.roll(a, HEAD_DIM, axis=1), b)))
        for g in range(KV_GROUPS):
            lhs = jnp.concatenate([pairs[0][g], pairs[1][g]], axis=1).astype(bf16)
            d = _dot(lhs, w4_ref[kind, s4])
            accs[g] = d if accs[g] is None else accs[g] + d
    return accs


def _cmp_hidden_kernel(xk_ref, xv_ref, wz_ref, h_ref):
    segs = h_ref.shape[0]
    hw = CMP_RATIO * CMP_HIDDEN
    for kind, x_ref in enumerate((xk_ref, xv_ref)):
        load = lambda s, x_ref=x_ref: x_ref[pl.ds(s, segs, stride=CMP_STRIDE), :]
        accs = _cmp_hidden_rows(load, wz_ref, kind)
        for g in range(KV_GROUPS):
            c0 = (kind * KV_GROUPS + g) * hw
            h_ref[:, c0:c0 + hw] = accs[g]


def _cmp_hidden(kv, wz):
    n = kv.shape[0]
    rows = CMP_SEGS * CMP_STRIDE
    assert n % rows == 0
    hw = 2 * KV_GROUPS * CMP_RATIO * CMP_HIDDEN
    return pl.pallas_call(
        _cmp_hidden_kernel,
        grid=(n // rows,),
        in_specs=[pl.BlockSpec((rows, LANES), lambda i: (i, 0)),
                  pl.BlockSpec((rows, LANES), lambda i: (i, 1)),
                  pl.BlockSpec(wz.shape, lambda i: (0,) * wz.ndim)],
        out_specs=pl.BlockSpec((CMP_SEGS, hw), lambda i: (i, 0)),
        out_shape=jax.ShapeDtypeStruct((n // CMP_STRIDE, hw), f32),
        compiler_params=pltpu.CompilerParams(dimension_semantics=("arbitrary",),
                                             vmem_limit_bytes=VMEM_LIMIT),
        name="cmp_hidden",
    )(kv, kv, wz)


def _cmp_bias_kernel(pe_ref, w1_ref, o_ref):
    for kind in range(2):
        o_ref[kind] = _dot(pe_ref[kind].astype(bf16), w1_ref[kind].astype(bf16))


def _cmp_bias(pe8, w1flat):
    return pl.pallas_call(
        _cmp_bias_kernel,
        out_shape=jax.ShapeDtypeStruct((2, SUBLANES, CMP_HIDDEN), f32),
        name="cmp_bias",
    )(pe8, w1flat)


def _cmp_finalize(h_refs, bias_ref, w2p_ref):
    hw = CMP_RATIO * CMP_HIDDEN
    outs = []
    for kind, h_ref in enumerate(h_refs):
        nseg = h_ref.shape[0]
        acc = None
        for g in range(KV_GROUPS):
            h0 = h_ref[:, g * hw:g * hw + CMP_HIDDEN]
            h1 = h_ref[:, g * hw + CMP_HIDDEN:(g + 1) * hw]
            hid = h0 + pltpu.roll(h1, nseg - 1, axis=0) + bias_ref[kind, 0:1, :]
            d = _dot(_gelu(hid).astype(bf16), w2p_ref[kind, g])
            acc = d if acc is None else acc + d
        outs.append(acc)
    return outs


def _cmp_final_kernel(h_ref, bias_ref, w2p_ref, kc_ref, vc_ref):
    kw = KV_GROUPS * CMP_RATIO * CMP_HIDDEN
    kc, vc = _cmp_finalize((h_ref.at[:, 0:kw], h_ref.at[:, kw:2 * kw]), bias_ref, w2p_ref)
    kc_ref[...] = kc.astype(bf16)
    vc_ref[...] = vc.astype(bf16)


def _cmp_final(hid, bias, w2p, batch):
    nseg = hid.shape[0] // batch
    hw = hid.shape[1]
    full = lambda a: pl.BlockSpec(a.shape, lambda b: (0,) * a.ndim)
    out = jax.ShapeDtypeStruct((batch, nseg, LANES), bf16)
    return pl.pallas_call(
        _cmp_final_kernel,
        grid=(batch,),
        in_specs=[pl.BlockSpec((nseg, hw), lambda b: (b, 0)), full(bias), full(w2p)],
        out_specs=[pl.BlockSpec((None, nseg, LANES), lambda b: (b, 0, 0))] * 2,
        out_shape=[out, out],
        compiler_params=pltpu.CompilerParams(dimension_semantics=("arbitrary",),
                                             vmem_limit_bytes=VMEM_LIMIT),
        name="cmp_final",
    )(hid, bias, w2p)


def _softmax_rows(s, mask):
    s = jnp.where(mask, s, NEG)
    m = jnp.max(s, axis=-1, keepdims=True)
    p = jnp.where(mask, jnp.exp2(s - m), 0.0)
    l = jnp.sum(p, axis=-1, keepdims=True)
    return p, 1.0 / jnp.maximum(l, 1e-30)


def _importance(psum, m):
    hi = psum.astype(bf16)
    lo = (psum - hi.astype(f32)).astype(bf16)
    return _dot(hi, m) + _dot(lo, m)


def _select_bias(score, forced, rounds):
    st = jnp.where(forced, -jnp.inf, score).T
    bias0 = jnp.where(forced, 0.0, NEG).T
    blk = lax.broadcasted_iota(jnp.int32, st.shape, 0).astype(f32)

    def body(_, carry):
        s, bias = carry
        m = jnp.max(s, axis=0, keepdims=True)
        idx = jnp.min(jnp.where(s == m, blk, float(st.shape[0])), axis=0, keepdims=True)
        pick = blk == idx
        return jnp.where(pick, -jnp.inf, s), jnp.where(pick, 0.0, bias)

    _, bias = lax.fori_loop(0, rounds, body, (st, bias0), unroll=True)
    return bias.T


def _with_ones(v):
    return jnp.concatenate([v, jnp.ones(v.shape, v.dtype)], axis=1)


def _attn_prompt_kernel(q_ref, gate_ref, kc_ref, vc_ref, ks_ref, vs_ref, kw_ref, vw_ref,
                        m_ref, e_ref, o_ref, acc_sc, m_sc, s0_sc, s1_sc, p0_sc, p1_sc, a0_sc, a1_sc,
                        *, nc):
    i = pl.program_id(2)
    qb = Q_BLOCK
    q0 = i * qb
    head = lambda h: slice(h * qb, (h + 1) * qb)
    q = jnp.concatenate([q_ref[:, h * LANES:(h + 1) * LANES] for h in range(HPG)], axis=0)
    qpos = q0 + lax.broadcasted_iota(jnp.int32, (qb, 1), 0)

    kc = kc_ref[...]
    nseg = kc.shape[0]
    n_idx = lax.broadcasted_iota(jnp.int32, (qb, nseg), 1)
    cmask = (n_idx * CMP_STRIDE + (CMP_LEN - 1) <= qpos) & (n_idx < nc)
    s_c = _dot_nt(q, kc)
    p_c = []
    psum = None
    for h in range(HPG):
        p, inv = _softmax_rows(s_c[head(h)], cmask)
        p = p * inv
        p_c.append(p.astype(bf16))
        psum = p if psum is None else psum + p
    o_c = _dot(jnp.concatenate(p_c, axis=0), vc_ref[...])

    wlen = WINDOW + qb
    w0 = pl.multiple_of(jnp.maximum(q0 - WINDOW, 0), qb)
    dist = qpos - (w0 + lax.broadcasted_iota(jnp.int32, (qb, wlen), 1))
    wmask = (dist >= 0) & (dist < WINDOW)
    s_w = _dot_nt(q, kw_ref[pl.ds(w0, wlen), :])
    p_w = []
    for h in range(HPG):
        s = jnp.where(wmask, s_w[head(h)], NEG)
        p_w.append(jnp.exp2(s - jnp.max(s, axis=-1, keepdims=True)).astype(bf16))
    o_w = _dot(jnp.concatenate(p_w, axis=0), _with_ones(vw_ref[pl.ds(w0, wlen), :]))

    imp = _importance(psum, m_ref[...])
    n_slc = imp.shape[1]
    blk = lax.broadcasted_iota(jnp.int32, (qb, n_slc), 1)
    cur = jnp.right_shift(q0 + lax.broadcasted_iota(jnp.int32, (qb, n_slc), 0), SLC_SHIFT)
    forced = (blk == 0) | (blk == cur) | (blk == cur - 1)
    score = jnp.where(blk <= cur, imp, -jnp.inf)
    sel_bias = _select_bias(score, forced, TOP_N - 3).astype(bf16)

    q_aug = jnp.concatenate([q, jnp.concatenate([sel_bias] * HPG, axis=0)], axis=1)
    acc_sc[...] = jnp.zeros(acc_sc.shape, f32)
    m_sc[...] = jnp.full(m_sc.shape, NEG, f32)
    kc_len = SEL_KC

    s_bufs = (s0_sc, s1_sc)
    p_bufs = (p0_sc, p1_sc)
    a_bufs = (a0_sc, a1_sc)

    def scores(c, par):
        k0 = pl.multiple_of(c * kc_len, kc_len)
        k_aug = jnp.concatenate([ks_ref[pl.ds(k0, kc_len), :], e_ref[pl.ds(k0, kc_len), :]], axis=1)
        s_bufs[par][...] = _dot_nt(q_aug, k_aug)

    def numerators(c, par, causal):
        if causal:
            k0 = c * kc_len
            ok = k0 + lax.broadcasted_iota(jnp.int32, (qb, kc_len), 1) <= qpos
        for h in range(HPG):
            s = s_bufs[par][head(h), :]
            if causal:
                s = jnp.where(ok, s, NEG)
            m_old = m_sc[head(h), :]
            m_new = jnp.maximum(m_old, jnp.max(s, axis=-1, keepdims=True))
            p_bufs[par][head(h), :] = jnp.exp2(s - m_new).astype(bf16)
            a_bufs[par][head(h), :] = jnp.exp2(m_old - m_new)
            m_sc[head(h), :] = m_new

    def value_sums(c, par):
        k0 = pl.multiple_of(c * kc_len, kc_len)
        pv = _dot(p_bufs[par][...], _with_ones(vs_ref[pl.ds(k0, kc_len), :]))
        acc_sc[...] = a_bufs[par][...] * acc_sc[...] + pv

    def stage(c, par):
        scores(c + 1, 1 - par)
        value_sums(jnp.maximum(c - 1, 0), 1 - par)
        numerators(c, par, False)

    def last_stage(c, par):
        value_sums(jnp.maximum(c - 1, 0), 1 - par)
        numerators(c, par, True)
        value_sums(c, par)

    p_bufs[1][...] = jnp.zeros(p_bufs[1].shape, bf16)
    a_bufs[1][...] = jnp.ones(a_bufs[1].shape, f32)

    def body(k, carry):
        stage(2 * k, 0)
        stage(2 * k + 1, 1)
        return carry

    c_diag = q0 // kc_len
    scores(0, 0)
    lax.fori_loop(0, c_diag // 2, body, 0)

    @pl.when(c_diag % 2 == 0)
    def _():
        last_stage(c_diag, 0)

    @pl.when(c_diag % 2 == 1)
    def _():
        stage(c_diag - 1, 0)
        last_stage(c_diag, 1)

    gt = gate_ref[...]
    for h in range(HPG):
        ow = o_w[head(h)]
        acc = acc_sc[head(h), :]
        c = h * N_GATES
        o = (o_c[head(h)] * gt[:, c:c + 1]
             + acc[:, :LANES] * (1.0 / acc[:, LANES:]) * gt[:, c + 1:c + 2]
             + ow[:, :LANES] * (1.0 / ow[:, LANES:]) * gt[:, c + 2:c + 3])
        o_ref[:, h * LANES:(h + 1) * LANES] = o.astype(o_ref.dtype)


def _attn_prompt(qpad, gates, kc, vc, kvb, imp_m, sel_e, batch):
    n = qpad.shape[0]
    t = n // batch
    qb = Q_BLOCK
    nqb = t // qb
    assert t % SEL_KC == 0 and t >= WINDOW + qb and t // SLC_BLOCK >= TOP_N
    nseg = t // CMP_STRIDE
    nc = nseg - CMP_RATIO + 1
    gw = HPG * LANES
    col = lambda c: pl.BlockSpec((t, LANES), lambda b, g, i, c=c: (b, c))
    cmp_spec = pl.BlockSpec((None, nseg, LANES), lambda b, g, i: (b, 0, 0))
    full = lambda a: pl.BlockSpec(a.shape, lambda b, g, i: (0,) * a.ndim)
    return pl.pallas_call(
        functools.partial(_attn_prompt_kernel, nc=nc),
        grid=(batch, KV_GROUPS, nqb),
        in_specs=[pl.BlockSpec((qb, gw), lambda b, g, i: (b * nqb + i, g)),
                  pl.BlockSpec((qb, LANES), lambda b, g, i: (b * nqb + i, g)),
                  cmp_spec, cmp_spec, col(2), col(3), col(4), col(5),
                  full(imp_m), full(sel_e)],
        out_specs=pl.BlockSpec((qb, gw), lambda b, g, i: (b * nqb + i, g)),
        out_shape=jax.ShapeDtypeStruct((n, NSA_HEADS * LANES), bf16),
        scratch_shapes=[pltpu.VMEM((HPG * qb, 2 * LANES), f32), pltpu.VMEM((HPG * qb, 1), f32),
                        pltpu.VMEM((HPG * qb, SEL_KC), f32), pltpu.VMEM((HPG * qb, SEL_KC), f32),
                        pltpu.VMEM((HPG * qb, SEL_KC), bf16), pltpu.VMEM((HPG * qb, SEL_KC), bf16),
                        pltpu.VMEM((HPG * qb, 1), f32), pltpu.VMEM((HPG * qb, 1), f32)],
        compiler_params=pltpu.CompilerParams(
            dimension_semantics=("arbitrary", "arbitrary", "arbitrary"),
            vmem_limit_bytes=VMEM_LIMIT),
        name="attn_prompt",
    )(qpad, gates, kc, vc, kvb, kvb, kvb, kvb, imp_m, sel_e)


def _attn_sample_kernel(pt_ref, cache_ref, q_ref, gate_ref, kv_ref, win_ref, perm_ref, wz_ref,
                        bias_ref, w2p_ref, m_ref, e_ref, o_ref, pages_sc, page_sem, xs0_sc, xs1_sc,
                        hk_sc, hv_sc, ks_sc, vs_sc, *, past, tdec, steps):
    npg = PAGES_PER_STEP
    j = pl.program_id(1)
    nj = pl.num_programs(1)
    segs_pp = PAGE_SIZE // CMP_STRIDE
    segs = npg * segs_pp
    hw = CMP_RATIO * CMP_HIDDEN
    xs_bufs = (xs0_sc, xs1_sc)
    h_bufs = (hk_sc, hv_sc)

    gstep = pl.program_id(0) * steps + j
    total_steps = pl.num_programs(0) * steps

    def page_copy(step, pi):
        slot = step % PAGE_SLOTS
        return pltpu.make_async_copy(cache_ref.at[pt_ref[step * npg + pi]],
                                     pages_sc.at[slot * npg + pi], page_sem.at[slot])

    def fetch(step):
        for pi in range(npg):
            page_copy(step, pi).start()

    @pl.when(gstep == 0)
    def _():
        for s in range(PAGE_SLOTS - 1):
            fetch(s)

    @pl.when(gstep + (PAGE_SLOTS - 1) < total_steps)
    def _():
        fetch(gstep + (PAGE_SLOTS - 1))

    for pi in range(npg):
        page_copy(gstep, pi).wait()
    slot0 = (gstep % PAGE_SLOTS) * npg
    pages = [pages_sc.at[slot0 + pi] for pi in range(npg)]

    @pl.when((pl.program_id(0) == 0) & (j == 0))
    def _():
        ks_sc[LANES:2 * LANES, :] = e_ref[...]

    def unpack_pages(xs_sc):
        perm = perm_ref[...]
        for pi, pg in enumerate(pages):
            xp = _dot_nt(perm, pg[0:2 * LANES, :].astype(bf16))
            for s in range(CMP_STRIDE):
                rows_s = xp[s * segs_pp:(s + 1) * segs_pp]
                for kind in range(2):
                    xs_sc[kind, s, pi * segs_pp:(pi + 1) * segs_pp, :] = (
                        rows_s[:, kind * LANES:(kind + 1) * LANES])
            c0 = pl.multiple_of((j * npg + pi) * PAGE_SIZE, PAGE_SIZE)
            ks_sc[0:LANES, pl.ds(c0, PAGE_SIZE)] = pg[2 * LANES:3 * LANES, :].astype(bf16)
            vs_sc[pl.ds(c0, PAGE_SIZE), :] = pg[3 * LANES:4 * LANES, :].T.astype(bf16)

    def compress_rows(xs_sc, step):
        seg0 = pl.multiple_of(step * segs, segs)
        for kind in range(2):
            load = lambda s, kind=kind: xs_sc[kind, s]
            accs = _cmp_hidden_rows(load, wz_ref, kind)
            for g in range(KV_GROUPS):
                h_bufs[kind][pl.ds(seg0, segs), g * hw:(g + 1) * hw] = accs[g]

    nh = NSA_HEADS
    rows = nh * tdec

    def queries():
        qf = q_ref[...]
        q = jnp.concatenate([qf[:, h * LANES:(h + 1) * LANES] for h in range(nh)],
                            axis=0).astype(bf16)
        return q, lax.broadcasted_iota(jnp.int32, (rows, 1), 0) & (tdec - 1)

    def new_rows(c):
        blk_new = kv_ref[:, c * LANES:(c + 1) * LANES].astype(bf16)
        return jnp.concatenate([blk_new, jnp.zeros((LANES - tdec, LANES), bf16)], axis=0)

    def two_part(q, trow, s_old, v_old, c_k, c_v, v_key_minor):
        new_mask = lax.broadcasted_iota(jnp.int32, (rows, LANES), 1) <= trow
        s_new = jnp.where(new_mask, _dot_nt(q, new_rows(c_k)), NEG)
        m = jnp.maximum(jnp.max(s_old, axis=-1, keepdims=True),
                        jnp.max(s_new, axis=-1, keepdims=True))
        p_old = jnp.exp2(s_old - m)
        p_new = jnp.exp2(s_new - m)
        l = jnp.sum(p_old, axis=-1, keepdims=True) + jnp.sum(p_new, axis=-1, keepdims=True)
        pv_old = (_dot_nt if v_key_minor else _dot)(p_old.astype(bf16), v_old)
        o = pv_old + _dot(p_new.astype(bf16), new_rows(c_v))
        return o * (1.0 / l)

    @pl.when(j == 0)
    def _():
        unpack_pages(xs_bufs[0])

    for parity in range(2):
        @pl.when((j > 0) & (j % 2 == parity))
        def _(parity=parity):
            unpack_pages(xs_bufs[parity])
            compress_rows(xs_bufs[1 - parity], j - 1)

    @pl.when(j == nj - 1)
    def _():
        compress_rows(xs_bufs[(steps - 1) % 2], j)
        nseg = past // CMP_STRIDE
        nc = (past + tdec) // CMP_STRIDE - CMP_RATIO + 1
        kc, vc = _cmp_finalize(h_bufs, bias_ref, w2p_ref)
        q, trow = queries()
        qpos = past + trow

        n_idx = lax.broadcasted_iota(jnp.int32, (rows, nseg), 1)
        cmask = (n_idx * CMP_STRIDE + (CMP_LEN - 1) <= qpos) & (n_idx < nc)
        p, inv = _softmax_rows(_dot_nt(q, kc.astype(bf16)), cmask)
        p = p * inv
        o_c = _dot(p.astype(bf16), vc.astype(bf16))

        gr = HPG * tdec
        psums = []
        for g in range(KV_GROUPS):
            ps = p[g * gr:g * gr + tdec]
            for h in range(1, HPG):
                ps = ps + p[g * gr + h * tdec:g * gr + (h + 1) * tdec]
            psums.append(ps)
        imp = _importance(jnp.concatenate(psums, axis=0), m_ref[...])
        n_past_blk = past // SLC_BLOCK
        blk = lax.broadcasted_iota(jnp.int32, imp.shape, 1)
        forced = (blk == 0) | (blk == n_past_blk - 1)
        pad = jnp.zeros((LANES - imp.shape[0], imp.shape[1]), f32)
        sel_bias = _select_bias(jnp.concatenate([imp, pad], axis=0),
                                jnp.concatenate([forced, pad > 0], axis=0), TOP_N - 3)
        q_aug = jnp.concatenate(
            [q, jnp.concatenate([sel_bias[g * tdec:(g + 1) * tdec] for g in range(KV_GROUPS)
                                 for _ in range(HPG)], axis=0).astype(bf16)], axis=1)

        o_s = two_part(q, trow, _dot(q_aug, ks_sc[...]), vs_sc[...], 2, 3, False)

        wb = win_ref.shape[1]
        dist = wb + trow - lax.broadcasted_iota(jnp.int32, (rows, wb), 1)
        s_w = _dot(q, win_ref[0:LANES, :].astype(bf16))
        s_w = jnp.where((dist >= 0) & (dist < WINDOW), s_w, NEG)
        o_w = two_part(q, trow, s_w, win_ref[LANES:2 * LANES, :].astype(bf16), 4, 5, True)

        gt = gate_ref[...]
        for h in range(nh):
            r0, r1 = h * tdec, (h + 1) * tdec
            c = (h // HPG) * LANES + (h % HPG) * N_GATES
            o = (o_c[r0:r1] * gt[:, c:c + 1] + o_s[r0:r1] * gt[:, c + 1:c + 2]
                 + o_w[r0:r1] * gt[:, c + 2:c + 3])
            o_ref[:, h * LANES:(h + 1) * LANES] = o.astype(o_ref.dtype)


def _attn_sample(page_table, cache, q3, gates3, kv3, win, perm, wz, bias, w2p, imp_m, sel_e):
    bd, n_pages = page_table.shape
    tdec = q3.shape[1]
    past = n_pages * PAGE_SIZE
    npg = PAGES_PER_STEP
    assert n_pages % npg == 0 and past % SLC_BLOCK == 0 and tdec <= CMP_STRIDE
    assert past // SLC_BLOCK == LANES and win.shape[2] <= WINDOW
    nj = n_pages // npg
    nseg = past // CMP_STRIDE
    hw = 2 * KV_GROUPS * CMP_RATIO * CMP_HIDDEN
    assert bd * nj >= PAGE_SLOTS - 1
    per_b = lambda a: pl.BlockSpec((None,) + a.shape[1:], lambda b, j, pt: (b,) + (0,) * (a.ndim - 1))
    full = lambda a: pl.BlockSpec(a.shape, lambda b, j, pt: (0,) * a.ndim)
    grid_spec = pltpu.PrefetchScalarGridSpec(
        num_scalar_prefetch=1,
        grid=(bd, nj),
        in_specs=[pl.BlockSpec(memory_space=pl.ANY), per_b(q3), per_b(gates3), per_b(kv3),
                  per_b(win), full(perm), full(wz), full(bias), full(w2p), full(imp_m), full(sel_e)],
        out_specs=pl.BlockSpec((None, tdec, NSA_HEADS * LANES), lambda b, j, pt: (b, 0, 0)),
        scratch_shapes=[pltpu.VMEM((PAGE_SLOTS * npg, 4 * KVW, PAGE_SIZE), f32),
                        pltpu.SemaphoreType.DMA((PAGE_SLOTS,)),
                        pltpu.VMEM((2, CMP_STRIDE, npg * PAGE_SIZE // CMP_STRIDE, LANES), f32),
                        pltpu.VMEM((2, CMP_STRIDE, npg * PAGE_SIZE // CMP_STRIDE, LANES), f32),
                        pltpu.VMEM((nseg, hw // 2), f32), pltpu.VMEM((nseg, hw // 2), f32),
                        pltpu.VMEM((2 * LANES, past), bf16),
                        pltpu.VMEM((past, LANES), bf16)],
    )
    return pl.pallas_call(
        functools.partial(_attn_sample_kernel, past=past, tdec=tdec, steps=nj),
        grid_spec=grid_spec,
        out_shape=jax.ShapeDtypeStruct((bd, tdec, NSA_HEADS * LANES), f32),
        compiler_params=pltpu.CompilerParams(dimension_semantics=("arbitrary", "arbitrary"),
                                             vmem_limit_bytes=VMEM_LIMIT),
        name="attn_sample",
    )(page_table.reshape(-1), cache, q3, gates3, kv3, win, perm, wz, bias, w2p, imp_m, sel_e)


def _ffn_rows_kernel(x_ref, a_ref, o_ref, wout_ref, g2_ref, gf_ref, wup_ref, cw_ref, cb_ref, wd_ref,
                     y_ref, tail_ref, carry_sc, *, tiles_per_seq):
    i = pl.program_id(0)
    tm = x_ref.shape[0]
    fc = FFN_FC

    @pl.when(i % tiles_per_seq == 0)
    def _():
        carry_sc[...] = jnp.zeros(carry_sc.shape, f32)

    cat = jnp.concatenate([a_ref[...], o_ref[...]], axis=1)
    hp = x_ref[...] + _dot(cat, wout_ref[...])
    h2 = ((hp * _rms_scale(hp)) * g2_ref[...]).astype(bf16)
    row = lax.broadcasted_iota(jnp.int32, (SUBLANES, fc), 0)

    def shifted(a, prev, k):
        rolled = pltpu.roll(a, k, axis=0)
        top = jnp.where(row < k, pltpu.roll(prev, k, axis=0), rolled[:SUBLANES])
        return jnp.concatenate([top, rolled[SUBLANES:]], axis=0)

    acc = None
    acts = []
    for j in range(D_FF // fc):
        halves = []
        for half in range(2):
            cols = slice(half * D_FF + j * fc, half * D_FF + (j + 1) * fc)
            a = _dot(h2, wup_ref[:, cols])
            tail = a[tm - SUBLANES:, :]
            tail_ref[:, cols] = tail
            prev = carry_sc[:, cols]
            s1 = shifted(a, prev, 1)
            s2 = shifted(a, prev, 2)
            carry_sc[:, cols] = tail
            halves.append(s2 * cw_ref[0:1, cols] + s1 * cw_ref[1:2, cols] + a * cw_ref[2:3, cols]
                          + cb_ref[:, cols])
        acts.append((_gelu(halves[0]) * halves[1]).astype(bf16))
        if len(acts) == FFN_DOWN_GROUP or j == D_FF // fc - 1:
            r1 = (j + 1) * fc
            d = _dot(jnp.concatenate(acts, axis=1), wd_ref[r1 - len(acts) * fc:r1, :])
            acc = d if acc is None else acc + d
            acts = []
    out = hp + acc
    y_ref[...] = (out * _rms_scale(out)) * gf_ref[...]


def _ffn_rows(x, a_out, o_pad, wout, g2, gf, wup, cw, cb, wdn, seq_len):
    n = x.shape[0]
    tm = FFN_TM
    assert n % tm == 0 and seq_len % tm == 0 and D_FF % FFN_FC == 0
    ni = n // tm
    row = lambda w: pl.BlockSpec((tm, w), lambda i: (i, 0))
    resident = lambda a: pl.BlockSpec(a.shape, lambda i: (0,) * a.ndim, pipeline_mode=pl.Buffered(1))
    return pl.pallas_call(
        functools.partial(_ffn_rows_kernel, tiles_per_seq=seq_len // tm),
        grid=(ni,),
        in_specs=[row(D_MODEL), row(GM_WIDTH), row(NSA_HEADS * LANES), resident(wout), resident(g2),
                  resident(gf), resident(wup), resident(cw), resident(cb), resident(wdn)],
        out_specs=[row(D_MODEL), pl.BlockSpec((None, SUBLANES, 2 * D_FF), lambda i: (i, 0, 0))],
        out_shape=[jax.ShapeDtypeStruct((n, D_MODEL), f32),
                   jax.ShapeDtypeStruct((ni, SUBLANES, 2 * D_FF), f32)],
        scratch_shapes=[pltpu.VMEM((SUBLANES, 2 * D_FF), f32)],
        compiler_params=pltpu.CompilerParams(dimension_semantics=("arbitrary",),
                                             vmem_limit_bytes=VMEM_LIMIT),
        name="ffn_prompt",
    )(x, a_out, o_pad, wout, g2, gf, wup, cw, cb, wdn)


def _ffn_kernel(*refs, tiles_per_seq, tdec):
    if tdec:
        (x_ref, a_ref, o_ref, wout_ref, g2_ref, gf_ref, wg_ref, wv_ref, cwg_ref, cwv_ref,
         cbg_ref, cbv_ref, wd_ref, pg_ref, pv_ref, y_ref, tg_ref, tv_ref,
         hp_sc, h2_sc, acc_sc) = refs
    else:
        (x_ref, a_ref, o_ref, wout_ref, g2_ref, gf_ref, wg_ref, wv_ref, cwg_ref, cwv_ref,
         cbg_ref, cbv_ref, wd_ref, y_ref, tg_ref, tv_ref,
         hp_sc, h2_sc, acc_sc, cg_sc, cv_sc) = refs
    i = pl.program_id(0)
    j = pl.program_id(1)
    nj = pl.num_programs(1)
    tm = x_ref.shape[0]

    @pl.when(j == 0)
    def _():
        cat = jnp.concatenate([a_ref[...], o_ref[...].astype(bf16)], axis=1)
        hp = x_ref[...] + _dot(cat, wout_ref[...])
        hp_sc[...] = hp
        h2_sc[...] = ((hp * _rms_scale(hp)) * g2_ref[...]).astype(bf16)
        acc_sc[...] = jnp.zeros(acc_sc.shape, f32)

    h2 = h2_sc[...]
    halves = []
    for w_ref, cw_ref, cb_ref, t_ref, extra in (
            (wg_ref, cwg_ref, cbg_ref, tg_ref, pg_ref if tdec else cg_sc),
            (wv_ref, cwv_ref, cbv_ref, tv_ref, pv_ref if tdec else cv_sc)):
        a = _dot(h2, w_ref[...])
        fc = a.shape[1]
        if tdec:
            nb = tm // tdec
            a3 = a.reshape(nb, tdec, fc)
            t_ref[...] = a3[:, tdec - (CONV_W - 1):, :]
            prev = extra[...]
            trow = lax.broadcasted_iota(jnp.int32, (nb, tdec, fc), 1)
            s1 = jnp.where(trow == 0, prev[:, 1:2, :], pltpu.roll(a3, 1, axis=1))
            s2 = jnp.where(trow == 0, prev[:, 0:1, :],
                           jnp.where(trow == 1, prev[:, 1:2, :], pltpu.roll(a3, 2, axis=1)))
            c = (s2 * cw_ref[0:1, :] + s1 * cw_ref[1:2, :] + a3 * cw_ref[2:3, :]
                 + cb_ref[...]).reshape(tm, fc)
        else:
            tail = a[tm - SUBLANES:, :]
            t_ref[...] = tail

            @pl.when(i % tiles_per_seq == 0)
            def _(extra=extra):
                extra[j] = jnp.zeros(extra.shape[1:], f32)

            prev = extra[j]
            row = lax.broadcasted_iota(jnp.int32, (tm, fc), 0)
            s1 = jnp.where(row == 0, prev[SUBLANES - 1:SUBLANES, :], pltpu.roll(a, 1, axis=0))
            s2 = jnp.where(row == 0, prev[SUBLANES - 2:SUBLANES - 1, :],
                           jnp.where(row == 1, prev[SUBLANES - 1:SUBLANES, :],
                                     pltpu.roll(a, 2, axis=0)))
            extra[j] = tail
            c = s2 * cw_ref[0:1, :] + s1 * cw_ref[1:2, :] + a * cw_ref[2:3, :] + cb_ref[...]
        halves.append(c)
    act = (_gelu(halves[0]) * halves[1]).astype(bf16)
    acc_sc[...] += _dot(act, wd_ref[...])

    @pl.when(j == nj - 1)
    def _():
        out = hp_sc[...] + acc_sc[...]
        y_ref[...] = (out * _rms_scale(out)) * gf_ref[...]


def _ffn(x, a_out, o_pad, wout, g2, gf, wup, cw, cb, wdn, state, seq_len, tdec, tm):
    n = x.shape[0]
    fc = FFN_FC
    assert n % tm == 0 and D_FF % fc == 0
    ni, nj = n // tm, D_FF // fc
    row = lambda w: pl.BlockSpec((tm, w), lambda i, j: (i, 0))
    full = lambda a: pl.BlockSpec(a.shape, lambda i, j: (0,) * a.ndim)
    gate_col = lambda r: pl.BlockSpec((r, fc), lambda i, j: (0, j))
    val_col = lambda r: pl.BlockSpec((r, fc), lambda i, j: (0, nj + j))
    in_specs = [row(D_MODEL), row(GM_WIDTH), row(NSA_HEADS * LANES), full(wout), full(g2), full(gf),
                gate_col(D_MODEL), val_col(D_MODEL), gate_col(CONV_W), val_col(CONV_W),
                gate_col(1), val_col(1), pl.BlockSpec((fc, D_MODEL), lambda i, j: (j, 0))]
    args = [x, a_out, o_pad, wout, g2, gf, wup, wup, cw, cw, cb, cb, wdn]
    scratch = [pltpu.VMEM((tm, D_MODEL), f32), pltpu.VMEM((tm, D_MODEL), bf16),
               pltpu.VMEM((tm, D_MODEL), f32)]
    if tdec:
        assert tm % tdec == 0 and tdec == SUBLANES
        nb = tm // tdec
        in_specs += [pl.BlockSpec((nb, CONV_W - 1, fc), lambda i, j: (i, 0, j)),
                     pl.BlockSpec((nb, CONV_W - 1, fc), lambda i, j: (i, 0, nj + j))]
        args += [state, state]
        tail_shape = jax.ShapeDtypeStruct((n // tdec, CONV_W - 1, D_FF), f32)
        tail_spec = pl.BlockSpec((nb, CONV_W - 1, fc), lambda i, j: (i, 0, j))
        tiles_per_seq = 0
    else:
        assert seq_len % tm == 0
        tiles_per_seq = seq_len // tm
        tail_shape = jax.ShapeDtypeStruct((ni, SUBLANES, D_FF), f32)
        tail_spec = pl.BlockSpec((None, SUBLANES, fc), lambda i, j: (i, 0, j))
        scratch += [pltpu.VMEM((nj, SUBLANES, fc), f32), pltpu.VMEM((nj, SUBLANES, fc), f32)]
    return pl.pallas_call(
        functools.partial(_ffn_kernel, tiles_per_seq=tiles_per_seq, tdec=tdec),
        grid=(ni, nj),
        in_specs=in_specs,
        out_specs=[row(D_MODEL), tail_spec, tail_spec],
        out_shape=[jax.ShapeDtypeStruct((n, D_MODEL), f32), tail_shape, tail_shape],
        scratch_shapes=scratch,
        compiler_params=pltpu.CompilerParams(dimension_semantics=("arbitrary", "arbitrary"),
                                             vmem_limit_bytes=VMEM_LIMIT),
        name="ffn_sample" if tdec else "ffn_prompt",
    )(*args)


def _pad_group_lanes(w, heads_axis_len):
    lead = w.shape[:-1]
    w = w.reshape(lead + (heads_axis_len, HEAD_DIM))
    z = jnp.zeros_like(w)
    first = (jnp.arange(heads_axis_len) < HPG)[:, None]
    lo = jnp.where(first, w, z)
    hi = jnp.where(first, z, w)
    return jnp.concatenate([lo, hi], axis=-1).reshape(lead + (heads_axis_len * LANES,))


def _importance_matrix(nseg, n_blk):
    m = np.zeros((nseg, n_blk), np.float32)
    for n in range(nseg - CMP_RATIO + 1):
        for r in range(CMP_RATIO):
            b = (n + r) // SEG_PER_SLC
            if b < n_blk:
                m[n, b] += 1.0
    return jnp.asarray(m, bf16)


def _block_one_hot(n_blk, n_keys):
    e = (np.arange(n_keys)[None, :] // SLC_BLOCK) == np.arange(n_blk)[:, None]
    return e.astype(np.float32)


def _segment_permutation():
    segs = PAGE_SIZE // CMP_STRIDE
    p = np.zeros((PAGE_SIZE, PAGE_SIZE), np.float32)
    for s in range(CMP_STRIDE):
        for n in range(segs):
            p[s * segs + n, n * CMP_STRIDE + s] = 1.0
    return jnp.asarray(p, bf16)


def kernel(x_prompt, x_sample, cache_kv, cache_win, state_conv, page_table, norm1_g, w_in, gm_ln_g,
           gm_ln_b, gm_ws, gm_bs, cmp_pe, cmp_w1, cmp_w2, w_out, norm2_g, w_up, conv_w, conv_b,
           w_down, final_g):
    depth = w_in.shape[0]
    assert depth == 1
    l = 0
    B, T, _ = x_prompt.shape
    Bd, Td, _ = x_sample.shape
    assert T % CHUNK == 0 and Td == SUBLANES and CHUNK % Td == 0

    wi = w_in[l]
    o_q = 2 * GM_WIDTH
    o_kv = o_q + NSA_WIDTH
    o_gl = o_kv + N_KV_BRANCH * KVW
    wuv = wi[:, :o_q].astype(bf16)
    wq = _pad_group_lanes(wi[:, o_q:o_kv] * (HEAD_DIM ** -0.5 * math.log2(math.e)),
                          NSA_HEADS).astype(bf16)
    wkv = wi[:, o_kv:o_gl].astype(bf16)
    gpg = HPG * N_GATES
    wgl = jnp.concatenate(
        [jnp.pad(wi[:, o_gl + g * gpg:o_gl + (g + 1) * gpg], ((0, 0), (0, LANES - gpg)))
         for g in range(KV_GROUPS)], axis=1).astype(bf16)
    g1 = norm1_g[l].reshape(1, D_MODEL)
    lng = gm_ln_g[l].reshape(1, GM_WIDTH)
    lnb = gm_ln_b[l].reshape(1, GM_WIDTH)
    causal = jnp.tril(jnp.ones((CHUNK, CHUNK), bool))
    wmix_p = jnp.where(causal, gm_ws[l], 0.0).astype(bf16)
    bmix_p = jnp.broadcast_to(gm_bs[l][:, :, None], (GM_GROUPS, CHUNK, GM_GD))
    reps = CHUNK // Td
    ws_d = jnp.where(causal[:Td, :Td], gm_ws[l][:, :Td, :Td], 0.0)
    eye = jnp.eye(reps, dtype=f32)
    wmix_s = (eye[None, :, None, :, None] * ws_d[:, None, :, None, :]).reshape(
        GM_GROUPS, CHUNK, CHUNK).astype(bf16)
    bmix_s = jnp.broadcast_to(jnp.tile(gm_bs[l][:, :Td], (1, reps))[:, :, None],
                              (GM_GROUPS, CHUNK, GM_GD))

    w1r = cmp_w1[l].reshape(2, CMP_RATIO, CMP_STRIDE, HEAD_DIM, CMP_HIDDEN)
    w1cat = jnp.concatenate([w1r[:, r] for r in range(CMP_RATIO)], axis=-1)
    wz = w1cat.reshape(2, CMP_STRIDE // 4, 4 * HEAD_DIM, CMP_RATIO * CMP_HIDDEN).astype(bf16)
    pe8 = jnp.broadcast_to(cmp_pe[l].reshape(2, 1, CMP_LEN * HEAD_DIM), (2, SUBLANES, CMP_LEN * HEAD_DIM))
    cbias = _cmp_bias(pe8, cmp_w1[l].reshape(2, CMP_LEN * HEAD_DIM, CMP_HIDDEN))
    w2 = cmp_w2[l]
    z2 = jnp.zeros_like(w2)
    w2p = jnp.stack([jnp.concatenate([w2, z2], axis=-1),
                     jnp.concatenate([z2, w2], axis=-1)], axis=1).astype(bf16)

    wo = w_out[l]
    wout = jnp.concatenate([wo[:GM_WIDTH], _pad_group_lanes(wo[GM_WIDTH:].T, NSA_HEADS).T],
                           axis=0).astype(bf16)
    g2 = norm2_g[l].reshape(1, D_MODEL)
    gf = final_g.reshape(1, D_MODEL)
    wup = w_up[l].astype(bf16)
    wdn = w_down[l].astype(bf16)
    cw = conv_w[l]
    cb = conv_b[l].reshape(1, 2 * D_FF)

    xp = x_prompt.reshape(B * T, D_MODEL)
    a_p, _, q_p, kv_p, kvb_p, gate_p = _project(xp, g1, wuv, wq, wkv, wgl, lng, lnb, wmix_p, bmix_p, bf16)
    hid_p = _cmp_hidden(kv_p, wz)
    kc_p, vc_p = _cmp_final(hid_p, cbias, w2p, B)
    imp_m = _importance_matrix(T // CMP_STRIDE, T // SLC_BLOCK)
    one_hot_t = jnp.asarray(_block_one_hot(T // SLC_BLOCK, T).T, bf16)
    o_p = _attn_prompt(q_p, gate_p, kc_p, vc_p, kvb_p, imp_m, one_hot_t, B)
    y_p, tail_p = _ffn_rows(xp, a_p, o_p, wout, g2, gf, wup, cw, cb, wdn, T)
    tiles = T // FFN_TM
    conv_prompt = tail_p.reshape(B, tiles, SUBLANES, 2 * D_FF)[:, tiles - 1, SUBLANES - (CONV_W - 1):]
    kv6 = kv_p.reshape(B, T, N_KV_BRANCH, KV_GROUPS, HEAD_DIM)
    wp = min(WINDOW, T)

    n_pages = page_table.shape[1]
    past = n_pages * PAGE_SIZE
    xs = x_sample.reshape(Bd * Td, D_MODEL)
    a_s, vn_s, q_s, kv_s, _, gate_s = _project(xs, g1, wuv, wq, wkv, wgl, lng, lnb, wmix_s, bmix_s, f32)
    cache = jnp.transpose(cache_kv[l], (0, 2, 3, 4, 1)).reshape(cache_kv.shape[1], 4 * KVW, PAGE_SIZE)
    wb = cache_win.shape[2]
    win = jnp.transpose(cache_win[l], (0, 2, 3, 4, 1)).reshape(Bd, 2 * KVW, wb)
    o_s = _attn_sample(page_table, cache, q_s.reshape(Bd, Td, -1), gate_s.reshape(Bd, Td, -1),
                       kv_s.reshape(Bd, Td, -1), win, _segment_permutation(), wz, cbias, w2p,
                       _importance_matrix(past // CMP_STRIDE, past // SLC_BLOCK),
                       jnp.asarray(_block_one_hot(past // SLC_BLOCK, past), bf16))
    tm_s = min(FFN_TM, Bd * Td)
    y_s, tg_s, tv_s = _ffn(xs, a_s, o_s.reshape(Bd * Td, -1), wout, g2, gf, wup, cw, cb, wdn,
                           state_conv[l], Td, Td, tm_s)
    kv6_s = kv_s.reshape(Bd, Td, N_KV_BRANCH, KV_GROUPS, HEAD_DIM)
    win_new = jnp.concatenate([cache_win[l], kv6_s[:, :, 4:]], axis=1)[:, Td:]

    return (y_p.reshape(B, T, D_MODEL),
            y_s.reshape(Bd, Td, D_MODEL),
            kv6[None, :, :, :4],
            kv6_s[None, :, :, :4],
            kv6[None, :, T - wp:, 4:],
            win_new[None],
            conv_prompt[None],
            jnp.concatenate([tg_s, tv_s], axis=-1)[None],
            vn_s.reshape(1, Bd, Td, GM_WIDTH))
```
